```python
import jax, jax.numpy as jnp
from jax import lax
import numpy as np

D_MODEL = 1024
BATCH = 16
SEQ = 2048
DEPTH = 2

HEAD_DIM = 64
N_HEADS = D_MODEL // HEAD_DIM
FOX_HEADS = N_HEADS // 2
SWA_HEADS = N_HEADS - FOX_HEADS
SWA_KV_HEADS = max(1, SWA_HEADS // 4)
SWA_WINDOW = 128
Q_BLOCK = 128
DSA_HEADS = N_HEADS
DSA_KV_HEADS = max(1, DSA_HEADS // 4)
IDX_HEADS = 8
IDX_DIM = 64
DSA_TOPK_MAX = 256
D_FF = 2816
N_EXPERTS = 8
TOP_K = 2
D_FF_EXPERT = 1408
ROPE_THETA = 10000.0
NORM_EPS = 1e-6
N_EVEN = (DEPTH + 1) // 2
N_ODD = DEPTH // 2

EVEN_SPLITS = (FOX_HEADS * HEAD_DIM, FOX_HEADS * HEAD_DIM, FOX_HEADS * HEAD_DIM, FOX_HEADS,
               SWA_HEADS * HEAD_DIM, SWA_KV_HEADS * HEAD_DIM, SWA_KV_HEADS * HEAD_DIM)
EVEN_IN = sum(EVEN_SPLITS)
ODD_SPLITS = (DSA_HEADS * HEAD_DIM, DSA_KV_HEADS * HEAD_DIM, DSA_KV_HEADS * HEAD_DIM,
              IDX_HEADS * IDX_DIM, IDX_DIM, IDX_HEADS)
ODD_IN = sum(ODD_SPLITS)

kernel_name = 'hybrid_fox_swa_dsa_moe'


def _split(t, sizes):
    return jnp.split(t, np.cumsum(sizes)[:-1].tolist(), axis=-1)


def rms_norm(x, g):
    xf = x.astype(jnp.float32)
    y = xf * lax.rsqrt(jnp.mean(xf * xf, axis=-1, keepdims=True) + NORM_EPS)
    return (y * g.astype(jnp.float32)).astype(x.dtype)


def ada_modulation(c, w, b):
    mod = jax.nn.silu(c) @ w + b
    return jnp.split(mod[:, None, :], 6, axis=-1)


def rope_tables(positions, dim):
    half = dim // 2
    inv_freq = ROPE_THETA ** (-jnp.arange(half, dtype=jnp.float32) / half)
    ang = positions.astype(jnp.float32)[..., None] * inv_freq
    return jnp.cos(ang)[:, :, None, :], jnp.sin(ang)[:, :, None, :]


def apply_rope(t, cos, sin):
    half = t.shape[-1] // 2
    tf = t.astype(jnp.float32)
    t1, t2 = tf[..., :half], tf[..., half:]
    return jnp.concatenate([t1 * cos - t2 * sin, t2 * cos + t1 * sin], axis=-1).astype(t.dtype)


def swiglu(h, w_gate, w_up, w_down):
    return (jax.nn.silu(h @ w_gate) * (h @ w_up)) @ w_down


def forgetting_attention(q, k, v, log_f):
    B, S, H, D = q.shape
    cum = jnp.cumsum(log_f.astype(jnp.float32), axis=1).transpose(0, 2, 1)
    kpos = jnp.arange(S)
    scale = D ** -0.5

    def block(i):
        start = i * Q_BLOCK
        qi = lax.dynamic_slice_in_dim(q, start, Q_BLOCK, axis=1)
        ci = lax.dynamic_slice_in_dim(cum, start, Q_BLOCK, axis=2)
        s = jnp.einsum('bqhd,bkhd->bhqk', qi, k, preferred_element_type=jnp.float32) * scale
        s = s + ci[..., :, None] - cum[..., None, :]
        qpos = start + jnp.arange(Q_BLOCK)
        s = jnp.where(kpos[None, :] <= qpos[:, None], s, -jnp.inf)
        p = jax.nn.softmax(s, axis=-1)
        return jnp.einsum('bhqk,bkhd->bqhd', p.astype(v.dtype), v)

    out = lax.map(block, jnp.arange(S // Q_BLOCK))
    return jnp.moveaxis(out, 0, 1).reshape(B, S, H, D)


def sliding_window_sink_attention(q, k, v, sinks):
    B, S, H, D = q.shape
    KVH = k.shape[2]
    G = H // KVH
    W = SWA_WINDOW
    nb = S // W
    qb = q.reshape(B, nb, W, KVH, G, D)

    def band_keys(t):
        tb = t.reshape(B, nb, W, KVH, D)
        prev = jnp.concatenate([jnp.zeros_like(tb[:, :1]), tb[:, :-1]], axis=1)
        return jnp.concatenate([prev, tb], axis=2)

    kk, vv = band_keys(k), band_keys(v)
    s = jnp.einsum('bnqhgd,bnshd->bnhgqs', qb, kk, preferred_element_type=jnp.float32) * D ** -0.5
    rel = (jnp.arange(W)[:, None] + W) - jnp.arange(2 * W)[None, :]
    band = (rel >= 0) & (rel < W)
    not_pad = (jnp.arange(nb)[:, None, None] > 0) | (jnp.arange(2 * W) >= W)[None, None, :]
    valid = band[None] & not_pad
    s = jnp.where(valid[None, :, None, None], s, -jnp.inf)
    sink = sinks.astype(jnp.float32).reshape(KVH, G)[None, None, :, :, None, None]
    m = jnp.maximum(jnp.max(s, axis=-1, keepdims=True), sink)
    e = jnp.exp(s - m)
    p = e / (jnp.sum(e, axis=-1, keepdims=True) + jnp.exp(sink - m))
    out = jnp.einsum('bnhgqs,bnshd->bnqhgd', p.astype(v.dtype), vv)
    return out.reshape(B, S, H, D)


def dsa_attention(q, k, v, iq, ik, iw):
    B, S, H, D = q.shape
    KVH = k.shape[2]
    G = H // KVH
    topk = min(DSA_TOPK_MAX, S // 4)
    kpos = jnp.arange(S)
    scale = D ** -0.5
    gather = jax.vmap(lambda tb, ib: tb[ib])

    def block(i):
        start = i * Q_BLOCK
        qi = lax.dynamic_slice_in_dim(q, start, Q_BLOCK, axis=1).reshape(B, Q_BLOCK, KVH, G, D)
        iqi = lax.dynamic_slice_in_dim(iq, start, Q_BLOCK, axis=1)
        iwi = lax.dynamic_slice_in_dim(iw, start, Q_BLOCK, axis=1).astype(jnp.float32)
        qpos = start + jnp.arange(Q_BLOCK)
        logits = jnp.einsum('bqhd,bsd->bqhs', iqi, ik, preferred_element_type=jnp.float32)
        score = jnp.einsum('bqh,bqhs->bqs', iwi, jax.nn.relu(logits))
        score = jnp.where((kpos[None, :] <= qpos[:, None])[None], score, -jnp.inf)
        _, idx = lax.top_k(score, topk)
        sel_valid = idx <= qpos[None, :, None]
        ks = gather(k, idx)
        vs = gather(v, idx)
        s = jnp.einsum('bqhgd,bqthd->bhgqt', qi, ks, preferred_element_type=jnp.float32) * scale
        s = jnp.where(sel_valid[:, None, None], s, -jnp.inf)
        p = jax.nn.softmax(s, axis=-1)
        o = jnp.einsum('bhgqt,bqthd->bqhgd', p.astype(v.dtype), vs)
        return o.reshape(B, Q_BLOCK, H, D)

    out = lax.map(block, jnp.arange(S // Q_BLOCK))
    return jnp.moveaxis(out, 0, 1).reshape(B, S, H, D)


def moe_swiglu(h, router_w, router_b, w_gate, w_up, w_down):
    B, S, Dm = h.shape
    t = h.reshape(B * S, Dm)
    logits = (t @ router_w).astype(jnp.float32) + router_b.astype(jnp.float32)
    top_vals, top_idx = lax.top_k(logits, TOP_K)
    gates = jax.nn.softmax(top_vals, axis=-1)
    combine = jnp.sum(jax.nn.one_hot(top_idx, N_EXPERTS, dtype=jnp.float32) * gates[..., None], axis=1)
    combine = combine.astype(h.dtype)
    out = jnp.zeros_like(t)
    for e in range(N_EXPERTS):
        out = out + combine[:, e:e + 1] * swiglu(t, w_gate[e], w_up[e], w_down[e])
    return out.reshape(B, S, Dm)


def even_layer(x, c, cos, sin, ada_w, ada_b, norm_mix, norm_ffn, w_in, forget_b, sinks, w_out,
               ffn_gate, ffn_up, ffn_down):
    B, S, _ = x.shape
    sh_m, sc_m, g_m, sh_f, sc_f, g_f = ada_modulation(c, ada_w, ada_b)
    h = rms_norm(x, norm_mix) * (1 + sc_m) + sh_m
    fq, fk, fv, fgl, sq, sk, sv = _split(h @ w_in, EVEN_SPLITS)
    heads = lambda t, n: t.reshape(B, S, n, HEAD_DIM)
    log_f = jax.nn.log_sigmoid((fgl + forget_b).astype(jnp.float32))
    o_fox = forgetting_attention(heads(fq, FOX_HEADS), heads(fk, FOX_HEADS), heads(fv, FOX_HEADS), log_f)
    o_swa = sliding_window_sink_attention(apply_rope(heads(sq, SWA_HEADS), cos, sin),
                                          apply_rope(heads(sk, SWA_KV_HEADS), cos, sin),
                                          heads(sv, SWA_KV_HEADS), sinks)
    mix = jnp.concatenate([o_fox.reshape(B, S, -1), o_swa.reshape(B, S, -1)], axis=-1) @ w_out
    x = x + g_m * mix
    h = rms_norm(x, norm_ffn) * (1 + sc_f) + sh_f
    return x + g_f * swiglu(h, ffn_gate, ffn_up, ffn_down)


def odd_layer(x, c, cos, sin, ada_w, ada_b, norm_mix, norm_ffn, w_in, w_out,
              router_w, router_b, exp_gate, exp_up, exp_down):
    B, S, _ = x.shape
    sh_m, sc_m, g_m, sh_f, sc_f, g_f = ada_modulation(c, ada_w, ada_b)
    h = rms_norm(x, norm_mix) * (1 + sc_m) + sh_m
    q, k, v, iq, ik, iw = _split(h @ w_in, ODD_SPLITS)
    q = apply_rope(q.reshape(B, S, DSA_HEADS, HEAD_DIM), cos, sin)
    k = apply_rope(k.reshape(B, S, DSA_KV_HEADS, HEAD_DIM), cos, sin)
    v = v.reshape(B, S, DSA_KV_HEADS, HEAD_DIM)
    iq = apply_rope(iq.reshape(B, S, IDX_HEADS, IDX_DIM), cos, sin)
    ik = apply_rope(ik.reshape(B, S, 1, IDX_DIM), cos, sin)[:, :, 0]
    iw = iw * (IDX_HEADS ** -0.5 * IDX_DIM ** -0.5)
    o = dsa_attention(q, k, v, iq, ik, iw)
    x = x + g_m * (o.reshape(B, S, -1) @ w_out)
    h = rms_norm(x, norm_ffn) * (1 + sc_f) + sh_f
    return x + g_f * moe_swiglu(h, router_w, router_b, exp_gate, exp_up, exp_down)


def setup_inputs(seed: int = 0) -> dict:
    key = jax.random.key(seed)
    ks = iter(jax.random.split(key, 40))
    D = D_MODEL

    def nrm(shape, s):
        return jax.random.normal(next(ks), shape, jnp.float32) * s

    x = nrm((BATCH, SEQ, D), 1.0)
    c = nrm((BATCH, D), 1.0)
    offset = jax.random.randint(next(ks), (BATCH, 1), 0, 4096, dtype=jnp.int32)
    positions = offset + jnp.arange(SEQ, dtype=jnp.int32)[None, :]
    return {
        'x': x, 'c': c, 'positions': positions,
        'e_ada_w': nrm((N_EVEN, D, 6 * D), 0.5 * D ** -0.5),
        'e_ada_b': nrm((N_EVEN, 6 * D), 0.02),
        'e_norm_mix': 1.0 + nrm((N_EVEN, D), 0.05),
        'e_norm_ffn': 1.0 + nrm((N_EVEN, D), 0.05),
        'e_w_in': nrm((N_EVEN, D, EVEN_IN), D ** -0.5),
        'e_forget_b': 3.0 + nrm((N_EVEN, FOX_HEADS), 1.0),
        'e_sinks': nrm((N_EVEN, SWA_HEADS), 1.0),
        'e_w_out': nrm((N_EVEN, D, D), D ** -0.5),
        'e_ffn_gate': nrm((N_EVEN, D, D_FF), D ** -0.5),
        'e_ffn_up': nrm((N_EVEN, D, D_FF), D ** -0.5),
        'e_ffn_down': nrm((N_EVEN, D_FF, D), D_FF ** -0.5),
        'o_ada_w': nrm((N_ODD, D, 6 * D), 0.5 * D ** -0.5),
        'o_ada_b': nrm((N_ODD, 6 * D), 0.02),
        'o_norm_mix': 1.0 + nrm((N_ODD, D), 0.05),
        'o_norm_ffn': 1.0 + nrm((N_ODD, D), 0.05),
        'o_w_in': nrm((N_ODD, D, ODD_IN), D ** -0.5),
        'o_w_out': nrm((N_ODD, D, D), D ** -0.5),
        'o_router_w': nrm((N_ODD, D, N_EXPERTS), D ** -0.5),
        'o_router_b': nrm((N_ODD, N_EXPERTS), 0.01),
        'o_exp_gate': nrm((N_ODD, N_EXPERTS, D, D_FF_EXPERT), D ** -0.5),
        'o_exp_up': nrm((N_ODD, N_EXPERTS, D, D_FF_EXPERT), D ** -0.5),
        'o_exp_down': nrm((N_ODD, N_EXPERTS, D_FF_EXPERT, D), D_FF_EXPERT ** -0.5),
        'final_norm': 1.0 + nrm((D,), 0.05),
    }


def reference(x, c, positions,
              e_ada_w, e_ada_b, e_norm_mix, e_norm_ffn, e_w_in, e_forget_b, e_sinks, e_w_out,
              e_ffn_gate, e_ffn_up, e_ffn_down,
              o_ada_w, o_ada_b, o_norm_mix, o_norm_ffn, o_w_in, o_w_out,
              o_router_w, o_router_b, o_exp_gate, o_exp_up, o_exp_down,
              final_norm):
    cos, sin = rope_tables(positions, HEAD_DIM)
    for layer in range(DEPTH):
        i = layer // 2
        if layer % 2 == 0:
            x = even_layer(x, c, cos, sin, e_ada_w[i], e_ada_b[i], e_norm_mix[i], e_norm_ffn[i],
                           e_w_in[i], e_forget_b[i], e_sinks[i], e_w_out[i],
                           e_ffn_gate[i], e_ffn_up[i], e_ffn_down[i])
        else:
            x = odd_layer(x, c, cos, sin, o_ada_w[i], o_ada_b[i], o_norm_mix[i], o_norm_ffn[i],
                          o_w_in[i], o_w_out[i], o_router_w[i], o_router_b[i],
                          o_exp_gate[i], o_exp_up[i], o_exp_down[i])
    return rms_norm(x, final_norm)
```

```python
import functools

import jax
import jax.numpy as jnp
import numpy as np
from jax import lax
from jax.experimental import pallas as pl
from jax.experimental.pallas import tpu as pltpu

HEAD_DIM = 64
LANES = 128
FOX_HEADS = 8
SWA_HEADS = 8
SWA_KV_HEADS = 2
SWA_WINDOW = 128
DSA_HEADS = 16
DSA_KV_HEADS = 4
IDX_HEADS = 8
DSA_TOPK_MAX = 256
N_EXPERTS = 8
ROPE_THETA = 10000.0
NORM_EPS = 1e-6
NEG = -1e30
INT_MIN = -2 ** 31
VMEM_LIMIT = 56 * 1024 * 1024

F32 = jnp.float32
BF16 = jnp.bfloat16


def _mm(a, b):
    return jnp.dot(a, b, preferred_element_type=F32)


def _nt(a, b):
    return lax.dot_general(a, b, (((1,), (1,)), ((), ())), preferred_element_type=F32)


def _params(*sem):
    return pltpu.CompilerParams(dimension_semantics=sem, vmem_limit_bytes=VMEM_LIMIT)


def _norm_mod(x, g, scale, shift):
    y = x * lax.rsqrt(jnp.mean(x * x, axis=-1, keepdims=True) + NORM_EPS)
    return (y * g) * (1.0 + scale) + shift


def _log_sigmoid(z):
    return jnp.minimum(z, 0.0) - jnp.log1p(jnp.exp(-jnp.abs(z)))


def _silu(z):
    return z * (1.0 / (1.0 + jnp.exp(-z)))


def _split3(v):
    hi = v.astype(BF16)
    r1 = v - hi.astype(F32)
    mid = r1.astype(BF16)
    lo = (r1 - mid.astype(F32)).astype(BF16)
    return hi, mid, lo


def _rope128(t, cos, sin_signed):
    lane = lax.broadcasted_iota(jnp.int32, t.shape, 1)
    first_half = (lane & (HEAD_DIM - 1)) < (HEAD_DIM // 2)
    partner = jnp.where(first_half, pltpu.roll(t, LANES - HEAD_DIM // 2, 1), pltpu.roll(t, HEAD_DIM // 2, 1))
    return t * cos + partner * sin_signed


def _rope_table_kernel(pos_ref, inv_ref, sign_ref, cos_ref, sin_ref):
    ang = pos_ref[...].astype(F32) * inv_ref[...]
    cos_ref[...] = jnp.cos(ang)
    sin_ref[...] = jnp.sin(ang) * sign_ref[...]


def _rope_tables(positions):
    T = positions.size
    tm = min(1024, T)
    half = HEAD_DIM // 2
    inv_freq = ROPE_THETA ** (-jnp.arange(half, dtype=F32) / half)
    inv128 = jnp.tile(inv_freq, LANES // half)[None, :]
    sign128 = jnp.tile(jnp.concatenate([-jnp.ones((half,), F32), jnp.ones((half,), F32)]), LANES // HEAD_DIM)[None, :]
    return pl.pallas_call(
        _rope_table_kernel,
        grid=(T // tm,),
        in_specs=[pl.BlockSpec((tm, 1), lambda i: (i, 0)),
                  pl.BlockSpec((1, LANES), lambda i: (0, 0)),
                  pl.BlockSpec((1, LANES), lambda i: (0, 0))],
        out_specs=[pl.BlockSpec((tm, LANES), lambda i: (i, 0))] * 2,
        out_shape=[jax.ShapeDtypeStruct((T, LANES), F32)] * 2,
        compiler_params=_params("arbitrary"),
        name="rope_tables",
    )(positions.reshape(T, 1), inv128, sign128)


def _ada_kernel(c_ref, w_ref, b_ref, o_ref):
    a = _silu(c_ref[...])
    o_ref[...] = jnp.dot(a, w_ref[...], preferred_element_type=F32, precision=lax.Precision.HIGHEST) + b_ref[...]


def _ada(c, w, b):
    B, D = c.shape
    N = w.shape[1]
    tn = N // 6
    mod = pl.pallas_call(
        _ada_kernel,
        grid=(N // tn,),
        in_specs=[pl.BlockSpec((B, D), lambda j: (0, 0)),
                  pl.BlockSpec((D, tn), lambda j: (0, j)),
                  pl.BlockSpec((1, tn), lambda j: (0, j))],
        out_specs=pl.BlockSpec((B, tn), lambda j: (0, j)),
        out_shape=jax.ShapeDtypeStruct((B, N), F32),
        compiler_params=_params("arbitrary"),
        name="ada_mod",
    )(c, w, b[None, :])
    return mod.reshape(B, 6, D)


E_FQ, E_FK, E_FV, E_SQ, E_SK, E_SV, E_END = 0, 512, 1024, 1536, 2048, 2176, 2304
ROWS16 = 16


def _even_proj_kernel(x_ref, g_ref, mod_ref, w_ref, wfg_ref, wfgt_ref, fbc_ref, fbr_ref, cos_ref, sin_ref,
                      fq_ref, fk_ref, fv_ref, cc_ref, cr_ref, sq_ref, sk_ref, sv_ref,
                      carc_ref, carr_ref, *, tiles_per_seq):
    i = pl.program_id(0)
    tm = x_ref.shape[0]

    @pl.when(i % tiles_per_seq == 0)
    def _():
        carc_ref[...] = jnp.zeros_like(carc_ref)
        carr_ref[...] = jnp.zeros_like(carr_ref)

    hb = _norm_mod(x_ref[...], g_ref[...], mod_ref[0, 1:2, :], mod_ref[0, 0:1, :]).astype(BF16)
    fq_ref[...] = _mm(hb, w_ref[:, E_FQ:E_FK]).astype(BF16)
    fk_ref[...] = _mm(hb, w_ref[:, E_FK:E_FV]).astype(BF16)
    fv_ref[...] = _mm(hb, w_ref[:, E_FV:E_SQ]).astype(BF16)
    cos = cos_ref[...]
    sin = sin_ref[...]
    sq = _mm(hb, w_ref[:, E_SQ:E_SK])
    for g in range((E_SK - E_SQ) // LANES):
        sq_ref[:, g * LANES:(g + 1) * LANES] = _rope128(sq[:, g * LANES:(g + 1) * LANES], cos, sin).astype(BF16)
    sk_ref[...] = _rope128(_mm(hb, w_ref[:, E_SK:E_SV]), cos, sin).astype(BF16)
    sv_ref[...] = _mm(hb, w_ref[:, E_SV:E_END]).astype(BF16)

    row = lax.broadcasted_iota(jnp.int32, (tm, tm), 0)
    col = lax.broadcasted_iota(jnp.int32, (tm, tm), 1)
    lower = jnp.where(col <= row, 1.0, 0.0).astype(BF16)
    upper = jnp.where(row <= col, 1.0, 0.0).astype(BF16)

    lf_c = _log_sigmoid(_mm(hb, wfg_ref[...]) + fbc_ref[...])
    h1, h2, h3 = _split3(lf_c)
    cum_c = _mm(lower, h1) + _mm(lower, h2) + _mm(lower, h3) + carc_ref[...]
    cc_ref[...] = cum_c
    carc_ref[...] = cum_c[tm - 1:tm, :]

    lf_r = _log_sigmoid(_nt(wfgt_ref[...], hb) + fbr_ref[...])
    r1, r2, r3 = _split3(lf_r)
    cum_r = _mm(r1, upper) + _mm(r2, upper) + _mm(r3, upper) + carr_ref[...]
    cr_ref[0] = cum_r
    carr_ref[...] = cum_r[:, tm - 1:tm]


def _even_proj(x2, g, mod, w_main, w_fg, w_fgt, fb_col, fb_row, cos_t, sin_t, S):
    T, D = x2.shape
    B = T // S
    tm = min(512, S)
    tps = S // tm
    tok = lambda w: pl.BlockSpec((tm, w), lambda i: (i, 0))
    const = lambda a: pl.BlockSpec(a.shape, lambda i: (0,) * a.ndim)
    outs = pl.pallas_call(
        functools.partial(_even_proj_kernel, tiles_per_seq=tps),
        grid=(T // tm,),
        in_specs=[tok(D), const(g), pl.BlockSpec((1, 6, D), lambda i: (i // tps, 0, 0)),
                  const(w_main), const(w_fg), const(w_fgt), const(fb_col), const(fb_row),
                  tok(LANES), tok(LANES)],
        out_specs=[tok(512), tok(512), tok(512), tok(LANES),
                   pl.BlockSpec((1, ROWS16, tm), lambda i: (i // tps, 0, i % tps)),
                   tok(512), tok(LANES), tok(LANES)],
        out_shape=[jax.ShapeDtypeStruct((T, 512), BF16)] * 3 + [
            jax.ShapeDtypeStruct((T, LANES), F32), jax.ShapeDtypeStruct((B, ROWS16, S), F32),
            jax.ShapeDtypeStruct((T, 512), BF16), jax.ShapeDtypeStruct((T, LANES), BF16),
            jax.ShapeDtypeStruct((T, LANES), BF16)],
        scratch_shapes=[pltpu.VMEM((1, LANES), F32), pltpu.VMEM((ROWS16, 1), F32)],
        compiler_params=_params("arbitrary"),
        name="even_proj",
    )(x2, g, mod, w_main, w_fg, w_fgt, fb_col, fb_row, cos_t, sin_t)
    return outs


def _pair_masks():
    lane = lax.broadcasted_iota(jnp.int32, (1, LANES), 1)
    return lane < HEAD_DIM


def _online_update(s, v2, m_ref, l_ref, acc_ref, slot):
    m_old = m_ref[slot]
    m_new = jnp.maximum(m_old, jnp.max(s, axis=1, keepdims=True))
    alpha = jnp.exp(m_old - m_new)
    p = jnp.exp(s - m_new)
    l_ref[slot] = alpha * l_ref[slot] + jnp.sum(p, axis=1, keepdims=True)
    acc_ref[slot] = alpha * acc_ref[slot] + _mm(p.astype(BF16), v2)
    m_ref[slot] = m_new


def _init_state(m_ref, l_ref, acc_ref):
    m_ref[...] = jnp.full_like(m_ref, NEG)
    l_ref[...] = jnp.zeros_like(l_ref)
    acc_ref[...] = jnp.zeros_like(acc_ref)


def _finish_pair(lo, l_ref, acc_ref):
    return jnp.where(lo, acc_ref[0] / l_ref[0], acc_ref[1] / l_ref[1])


def _fox_kernel(q_ref, k_ref, v_ref, cc_ref, cr_ref, o_ref, m_ref, l_ref, acc_ref, *, tk):
    tq = q_ref.shape[0]
    q0 = pl.program_id(1) * tq
    nkv = (q0 + tq + tk - 1) // tk
    lo = _pair_masks()
    qpos = q0 + lax.broadcasted_iota(jnp.int32, (tq, 1), 0)
    for p in range(q_ref.shape[1] // LANES):
        cols = slice(p * LANES, (p + 1) * LANES)
        q2 = q_ref[:, cols]
        qa = jnp.where(lo, q2, jnp.zeros_like(q2))
        qb = jnp.where(lo, jnp.zeros_like(q2), q2)
        cqa = cc_ref[:, 2 * p:2 * p + 1]
        cqb = cc_ref[:, 2 * p + 1:2 * p + 2]
        _init_state(m_ref, l_ref, acc_ref)

        def body(j, carry):
            start = pl.multiple_of(j * tk, tk)
            k2 = k_ref[pl.ds(start, tk), cols]
            v2 = v_ref[pl.ds(start, tk), cols]
            causal = (start + lax.broadcasted_iota(jnp.int32, (1, tk), 1)) <= qpos
            cka = cr_ref[0, 2 * p:2 * p + 1, pl.ds(start, tk)]
            ckb = cr_ref[0, 2 * p + 1:2 * p + 2, pl.ds(start, tk)]
            sa = jnp.where(causal, _nt(qa, k2) + (cqa - cka), NEG)
            _online_update(sa, v2, m_ref, l_ref, acc_ref, 0)
            sb = jnp.where(causal, _nt(qb, k2) + (cqb - ckb), NEG)
            _online_update(sb, v2, m_ref, l_ref, acc_ref, 1)
            return carry

        lax.fori_loop(0, nkv, body, 0)
        o_ref[:, cols] = _finish_pair(lo, l_ref, acc_ref).astype(BF16)


def _attn_scratch(tq):
    return [pltpu.VMEM((2, tq, 1), F32), pltpu.VMEM((2, tq, 1), F32), pltpu.VMEM((2, tq, LANES), F32)]


def _fox_attention(fq, fk, fv, cum_c, cum_r, S):
    T, W = fq.shape
    B = T // S
    tq = min(256, S)
    tk = min(512, S)
    nq = S // tq
    return pl.pallas_call(
        functools.partial(_fox_kernel, tk=tk),
        grid=(B, nq),
        in_specs=[pl.BlockSpec((tq, W), lambda b, i: (b * nq + i, 0)),
                  pl.BlockSpec((S, W), lambda b, i: (b, 0)),
                  pl.BlockSpec((S, W), lambda b, i: (b, 0)),
                  pl.BlockSpec((tq, LANES), lambda b, i: (b * nq + i, 0)),
                  pl.BlockSpec((1, ROWS16, S), lambda b, i: (b, 0, 0))],
        out_specs=pl.BlockSpec((tq, W), lambda b, i: (b * nq + i, 0)),
        out_shape=jax.ShapeDtypeStruct((T, W), BF16),
        scratch_shapes=_attn_scratch(tq),
        compiler_params=_params("arbitrary", "arbitrary"),
        name="fox_attention",
    )(fq, fk, fv, cum_c, cum_r)


def _swa_kernel(sink_ref, q_ref, k_ref, v_ref, o_ref):
    tq = q_ref.shape[0]
    S = k_ref.shape[0]
    span = min(tq + SWA_WINDOW, S)
    q0 = pl.program_id(1) * tq
    start = pl.multiple_of(jnp.maximum(q0 - SWA_WINDOW, 0), SWA_WINDOW)
    k2 = k_ref[pl.ds(start, span), :]
    v2 = v_ref[pl.ds(start, span), :]
    lo = _pair_masks()
    qpos = q0 + lax.broadcasted_iota(jnp.int32, (tq, 1), 0)
    kpos = start + lax.broadcasted_iota(jnp.int32, (1, span), 1)
    valid = (kpos <= qpos) & (qpos - kpos < SWA_WINDOW)
    for p in range(q_ref.shape[1] // LANES):
        cols = slice(p * LANES, (p + 1) * LANES)
        q2 = q_ref[:, cols]
        halves = []
        for half, qh in enumerate((jnp.where(lo, q2, jnp.zeros_like(q2)), jnp.where(lo, jnp.zeros_like(q2), q2))):
            sink = sink_ref[2 * p + half]
            s = jnp.where(valid, _nt(qh, k2), NEG)
            m = jnp.maximum(jnp.max(s, axis=1, keepdims=True), sink)
            e = jnp.exp(s - m)
            den = jnp.sum(e, axis=1, keepdims=True) + jnp.exp(sink - m)
            halves.append(_mm(e.astype(BF16), v2) / den)
        o_ref[:, cols] = jnp.where(lo, halves[0], halves[1]).astype(BF16)


def _swa_attention(sq, sk, sv, sinks, S):
    T, W = sq.shape
    B = T // S
    tq = min(256, S)
    nq = S // tq
    return pl.pallas_call(
        _swa_kernel,
        grid=(B, nq),
        in_specs=[pl.BlockSpec(memory_space=pltpu.SMEM),
                  pl.BlockSpec((tq, W), lambda b, i: (b * nq + i, 0)),
                  pl.BlockSpec((S, LANES), lambda b, i: (b, 0)),
                  pl.BlockSpec((S, LANES), lambda b, i: (b, 0))],
        out_specs=pl.BlockSpec((tq, W), lambda b, i: (b * nq + i, 0)),
        out_shape=jax.ShapeDtypeStruct((T, W), BF16),
        compiler_params=_params("arbitrary", "arbitrary"),
        name="swa_attention",
    )(sinks, sq, sk, sv)


RANK_BLOCK = 256

def _post_attn_kernel(*refs, n_in, route):
    x_ref, mod_ref, g_ref = refs[0:3]
    o_refs = refs[3:3 + n_in]
    w_refs = refs[3 + n_in:3 + 2 * n_in]
    pos = 3 + 2 * n_in
    if route:
        rw_ref, rb_ref = refs[pos:pos + 2]
        pos += 2
    x1_ref, h_ref = refs[pos:pos + 2]
    pos += 2
    mix = _mm(o_refs[0][...], w_refs[0][...])
    for o_ref, w_ref in zip(o_refs[1:], w_refs[1:]):
        mix = mix + _mm(o_ref[...], w_ref[...])
    x1 = x_ref[...] + mod_ref[0, 2:3, :] * mix
    x1_ref[...] = x1
    h = _norm_mod(x1, g_ref[...], mod_ref[0, 4:5, :], mod_ref[0, 3:4, :])
    h_ref[...] = h.astype(BF16)
    if not route:
        return
    comb_ref, rankc_ref, rankr_ref, cnt_ref = refs[pos:pos + 4]
    tm = h.shape[0]
    E = rw_ref.shape[0]
    lane = lax.broadcasted_iota(jnp.int32, (tm, LANES), 1).astype(F32)
    logits = jnp.full((tm, LANES), -jnp.inf, F32)
    for e in range(E):
        le = jnp.sum(h * rw_ref[e:e + 1, :], axis=1, keepdims=True) + rb_ref[e]
        logits = jnp.where(lane == float(e), le, logits)
    top1 = jnp.max(logits, axis=1, keepdims=True)
    idx1 = jnp.min(jnp.where(logits == top1, lane, float(LANES)), axis=1, keepdims=True)
    rest = jnp.where(lane == idx1, -jnp.inf, logits)
    top2 = jnp.max(rest, axis=1, keepdims=True)
    idx2 = jnp.min(jnp.where(rest == top2, lane, float(LANES)), axis=1, keepdims=True)
    e2 = jnp.exp(top2 - top1)
    g1 = 1.0 / (1.0 + e2)
    g2 = e2 / (1.0 + e2)
    is1 = lane == idx1
    is2 = lane == idx2
    comb_ref[...] = jnp.where(is1, g1, 0.0) + jnp.where(is2, g2, 0.0)
    sel = jnp.where(is1 | is2, 1.0, 0.0)
    rb = min(RANK_BLOCK, tm)
    row = lax.broadcasted_iota(jnp.int32, (rb, rb), 0)
    col = lax.broadcasted_iota(jnp.int32, (rb, rb), 1)
    strict_lower = jnp.where(col < row, 1.0, 0.0).astype(BF16)
    seen = jnp.zeros((1, LANES), F32)
    for blk in range(tm // rb):
        sel_b = sel[blk * rb:(blk + 1) * rb, :]
        rank = _mm(strict_lower, sel_b.astype(BF16)) + seen
        rankc_ref[blk * rb:(blk + 1) * rb, :] = jnp.where(sel_b > 0.0, rank, -1.0)
        seen = seen + jnp.sum(sel_b, axis=0, keepdims=True)
    rankr_ref[...] = jnp.transpose(rankc_ref[...])[0:rankr_ref.shape[0], :]
    cnt_ref[0] = seen.astype(jnp.int32)


def _post_attn(x2, mod, g, os_, ws, S, tm, router=None):
    T, D = x2.shape
    tps = S // tm
    n_in = len(os_)
    tok = lambda w: pl.BlockSpec((tm, w), lambda i: (i, 0))
    const = lambda a: pl.BlockSpec(a.shape, lambda i: (0,) * a.ndim)
    in_specs = [tok(D), pl.BlockSpec((1, 6, D), lambda i: (i // tps, 0, 0)), const(g)]
    in_specs += [tok(o.shape[1]) for o in os_] + [const(w) for w in ws]
    args = [x2, mod, g, *os_, *ws]
    out_specs = [tok(D), tok(D)]
    out_shape = [jax.ShapeDtypeStruct((T, D), F32), jax.ShapeDtypeStruct((T, D), BF16)]
    if router is not None:
        rw_t, rb = router
        in_specs += [const(rw_t), pl.BlockSpec(memory_space=pltpu.SMEM)]
        args += [rw_t, rb]
        nT = T // tm
        out_specs += [tok(LANES), tok(LANES), pl.BlockSpec((8, tm), lambda i: (0, i)),
                      pl.BlockSpec((1, 1, LANES), lambda i: (i, 0, 0))]
        out_shape += [jax.ShapeDtypeStruct((T, LANES), F32), jax.ShapeDtypeStruct((T, LANES), F32),
                      jax.ShapeDtypeStruct((8, T), F32), jax.ShapeDtypeStruct((nT, 1, LANES), jnp.int32)]
    return pl.pallas_call(
        functools.partial(_post_attn_kernel, n_in=n_in, route=router is not None),
        grid=(T // tm,),
        in_specs=in_specs,
        out_specs=out_specs,
        out_shape=out_shape,
        compiler_params=_params("arbitrary"),
        name="post_attn_route" if router is not None else "post_attn",
    )(*args)


def _ffn_kernel(x1_ref, h_ref, mod_ref, wg_ref, wu_ref, wd_ref, o_ref, acc_ref):
    k = pl.program_id(1)
    h = h_ref[...]
    act = (_silu(_mm(h, wg_ref[...])) * _mm(h, wu_ref[...])).astype(BF16)
    part = _mm(act, wd_ref[...])

    @pl.when(k == 0)
    def _():
        acc_ref[...] = part

    @pl.when(k > 0)
    def _():
        acc_ref[...] += part

    @pl.when(k == pl.num_programs(1) - 1)
    def _():
        o_ref[...] = x1_ref[...] + mod_ref[0, 5:6, :] * acc_ref[...]


def _ffn(x1, h, mod, wg, wu, wd, S):
    T, D = x1.shape
    F = wg.shape[1]
    tm = min(512, S)
    tps = S // tm
    nf = 2
    tf = F // nf
    return pl.pallas_call(
        _ffn_kernel,
        grid=(T // tm, nf),
        in_specs=[pl.BlockSpec((tm, D), lambda i, k: (i, 0)),
                  pl.BlockSpec((tm, D), lambda i, k: (i, 0)),
                  pl.BlockSpec((1, 6, D), lambda i, k: (i // tps, 0, 0)),
                  pl.BlockSpec((D, tf), lambda i, k: (0, k)),
                  pl.BlockSpec((D, tf), lambda i, k: (0, k)),
                  pl.BlockSpec((tf, D), lambda i, k: (k, 0))],
        out_specs=pl.BlockSpec((tm, D), lambda i, k: (i, 0)),
        out_shape=jax.ShapeDtypeStruct((T, D), F32),
        scratch_shapes=[pltpu.VMEM((tm, D), F32)],
        compiler_params=_params("arbitrary", "arbitrary"),
        name="dense_ffn",
    )(x1, h, mod, wg, wu, wd)


O_Q, O_K, O_V, O_IQ, O_IK, O_END = 0, 1024, 1280, 1536, 2048, 2176
IW_SCALE = IDX_HEADS ** -0.5 * HEAD_DIM ** -0.5


def _odd_proj_kernel(x_ref, g_ref, mod_ref, w_ref, wiw_ref, cos_ref, sin_ref,
                     q_ref, k_ref, v_ref, iq_ref, ik_ref, iw_ref):
    hb = _norm_mod(x_ref[...], g_ref[...], mod_ref[0, 1:2, :], mod_ref[0, 0:1, :]).astype(BF16)
    cos = cos_ref[...]
    sin = sin_ref[...]

    def roped(lo_col, hi_col, out_ref):
        t = _mm(hb, w_ref[:, lo_col:hi_col])
        for g in range((hi_col - lo_col) // LANES):
            out_ref[:, g * LANES:(g + 1) * LANES] = _rope128(t[:, g * LANES:(g + 1) * LANES], cos, sin).astype(BF16)

    roped(O_Q, O_K, q_ref)
    roped(O_K, O_V, k_ref)
    v_ref[...] = _mm(hb, w_ref[:, O_V:O_IQ]).astype(BF16)
    roped(O_IQ, O_IK, iq_ref)
    roped(O_IK, O_END, ik_ref)
    iw_ref[...] = _mm(hb, wiw_ref[...]) * IW_SCALE


def _odd_proj(x2, g, mod, w_main, w_iw, cos_t, sin_t, S):
    T, D = x2.shape
    tm = min(512, S)
    tps = S // tm
    tok = lambda w: pl.BlockSpec((tm, w), lambda i: (i, 0))
    const = lambda a: pl.BlockSpec(a.shape, lambda i: (0,) * a.ndim)
    widths = [O_K - O_Q, O_V - O_K, O_IQ - O_V, O_IK - O_IQ, O_END - O_IK]
    return pl.pallas_call(
        _odd_proj_kernel,
        grid=(T // tm,),
        in_specs=[tok(D), const(g), pl.BlockSpec((1, 6, D), lambda i: (i // tps, 0, 0)),
                  const(w_main), const(w_iw), tok(LANES), tok(LANES)],
        out_specs=[tok(w) for w in widths] + [tok(LANES)],
        out_shape=[jax.ShapeDtypeStruct((T, w), BF16) for w in widths] + [jax.ShapeDtypeStruct((T, LANES), F32)],
        compiler_params=_params("arbitrary"),
        name="odd_proj",
    )(x2, g, mod, w_main, w_iw, cos_t, sin_t)


def _dsa_kernel(q_ref, k_ref, v_ref, iq_ref, ik_ref, iw_ref, o_ref,
                key_ref, bias_ref, m_ref, l_ref, acc_ref, *, tk, topk):
    tq = q_ref.shape[0]
    q0 = pl.program_id(1) * tq
    nkv = (q0 + tq + tk - 1) // tk
    lo = _pair_masks()
    qpos = q0 + lax.broadcasted_iota(jnp.int32, (tq, 1), 0)
    kf = float(topk)

    def score_body(j, carry):
        start = pl.multiple_of(j * tk, tk)
        ik2 = ik_ref[pl.ds(start, tk), :]
        sc = jnp.zeros((tq, tk), F32)
        for hp in range(iq_ref.shape[1] // LANES):
            iq2 = iq_ref[:, hp * LANES:(hp + 1) * LANES]
            la = _nt(jnp.where(lo, iq2, jnp.zeros_like(iq2)), ik2)
            sc = sc + iw_ref[:, 2 * hp:2 * hp + 1] * jnp.maximum(la, 0.0)
            lb = _nt(jnp.where(lo, jnp.zeros_like(iq2), iq2), ik2)
            sc = sc + iw_ref[:, 2 * hp + 1:2 * hp + 2] * jnp.maximum(lb, 0.0)
        sc = jnp.where(sc == 0.0, 0.0, sc)
        bits = lax.bitcast_convert_type(sc, jnp.int32)
        key = jnp.where(bits < 0, bits ^ jnp.int32(0x7FFFFFFF), bits)
        causal = (start + lax.broadcasted_iota(jnp.int32, (1, tk), 1)) <= qpos
        key_ref[:, pl.ds(start, tk)] = jnp.where(causal, key, jnp.int32(INT_MIN))
        return carry

    lax.fori_loop(0, nkv, score_body, 0)

    def count(pred):
        def body(j, acc):
            kc = key_ref[:, pl.ds(pl.multiple_of(j * tk, tk), tk)]
            return acc + jnp.sum(jnp.where(pred(kc), 1.0, 0.0), axis=1, keepdims=True)
        return lax.fori_loop(0, nkv, body, jnp.zeros((tq, 1), F32))

    zero = jnp.zeros((tq, 1), jnp.int32)
    thr = jnp.where(count(lambda kc: kc >= zero) >= kf, zero, jnp.int32(INT_MIN))

    def bit_body(it, thr):
        cand = thr + jnp.left_shift(jnp.int32(1), 30 - it)
        return jnp.where(count(lambda kc: kc >= cand) >= kf, cand, thr)

    thr = lax.fori_loop(0, 31, bit_body, thr)
    n_gt = count(lambda kc: kc > thr)
    n_ge = count(lambda kc: kc >= thr)
    need = kf - n_gt
    has_tie = jnp.max(jnp.where((n_ge > kf) & (thr != jnp.int32(INT_MIN)), 1.0, 0.0)) > 0.0

    @pl.when(jnp.logical_not(has_tie))
    def _():
        def body(j, carry):
            start = pl.multiple_of(j * tk, tk)
            kc = key_ref[:, pl.ds(start, tk)]
            sel = (kc >= thr) & (kc > jnp.int32(INT_MIN))
            bias_ref[:, pl.ds(start, tk)] = jnp.where(sel, 0.0, NEG)
            return carry
        lax.fori_loop(0, nkv, body, 0)

    @pl.when(has_tie)
    def _():
        r = lax.broadcasted_iota(jnp.int32, (tk, tk), 0)
        c = lax.broadcasted_iota(jnp.int32, (tk, tk), 1)
        strict_upper = jnp.where(r < c, 1.0, 0.0).astype(BF16)

        def body(j, seen):
            start = pl.multiple_of(j * tk, tk)
            kc = key_ref[:, pl.ds(start, tk)]
            eq = kc == thr
            eqf = jnp.where(eq, 1.0, 0.0)
            before = _mm(eqf.astype(BF16), strict_upper) + seen
            sel = ((kc > thr) | (eq & (before < need))) & (kc > jnp.int32(INT_MIN))
            bias_ref[:, pl.ds(start, tk)] = jnp.where(sel, 0.0, NEG)
            return seen + jnp.sum(eqf, axis=1, keepdims=True)
        lax.fori_loop(0, nkv, body, jnp.zeros((tq, 1), F32))

    pairs_per_block = (q_ref.shape[1] // LANES) // (k_ref.shape[1] // LANES)
    for p in range(q_ref.shape[1] // LANES):
        cols = slice(p * LANES, (p + 1) * LANES)
        kcols = slice((p // pairs_per_block) * LANES, (p // pairs_per_block + 1) * LANES)
        q2 = q_ref[:, cols]
        qa = jnp.where(lo, q2, jnp.zeros_like(q2))
        qb = jnp.where(lo, jnp.zeros_like(q2), q2)
        _init_state(m_ref, l_ref, acc_ref)

        def body(j, carry):
            start = pl.multiple_of(j * tk, tk)
            k2 = k_ref[pl.ds(start, tk), kcols]
            v2 = v_ref[pl.ds(start, tk), kcols]
            bias = bias_ref[:, pl.ds(start, tk)]
            _online_update(_nt(qa, k2) + bias, v2, m_ref, l_ref, acc_ref, 0)
            _online_update(_nt(qb, k2) + bias, v2, m_ref, l_ref, acc_ref, 1)
            return carry

        lax.fori_loop(0, nkv, body, 0)
        o_ref[:, cols] = _finish_pair(lo, l_ref, acc_ref).astype(BF16)


def _dsa_attention(q, k, v, iq, ik, iw, S):
    T, W = q.shape
    B = T // S
    tq = min(128, S)
    tk = min(512, S)
    nq = S // tq
    topk = min(DSA_TOPK_MAX, S // 4)
    qtok = lambda w: pl.BlockSpec((tq, w), lambda b, i: (b * nq + i, 0))
    seq = lambda w: pl.BlockSpec((S, w), lambda b, i: (b, 0))
    return pl.pallas_call(
        functools.partial(_dsa_kernel, tk=tk, topk=topk),
        grid=(B, nq),
        in_specs=[qtok(W), seq(k.shape[1]), seq(v.shape[1]), qtok(iq.shape[1]), seq(LANES), qtok(LANES)],
        out_specs=qtok(W),
        out_shape=jax.ShapeDtypeStruct((T, W), BF16),
        scratch_shapes=[pltpu.VMEM((tq, S), jnp.int32), pltpu.VMEM((tq, S), F32)] + _attn_scratch(tq),
        compiler_params=_params("arbitrary", "arbitrary"),
        name="dsa_attention",
    )(q, k, v, iq, ik, iw)


MOE_ROWS = 128


def _moe_kernel(cnt_ref, x1_ref, h_ref, comb_ref, rankc_ref, rankr_ref, mod_ref, wgu_ref, wd_ref, fn_ref, o_ref):
    i = pl.program_id(0)
    e = pl.program_id(1)
    E = pl.num_programs(1)
    tm = h_ref.shape[0]
    F = wd_ref.shape[1]

    @pl.when(e == 0)
    def _():
        o_ref[...] = x1_ref[...]

    lane = lax.broadcasted_iota(jnp.int32, (tm, LANES), 1)
    mine = lane == e
    rank_col = jnp.sum(jnp.where(mine, rankc_ref[...], 0.0), axis=1, keepdims=True)
    gate_col = jnp.sum(jnp.where(mine, comb_ref[...], 0.0), axis=1, keepdims=True)
    rank_row = rankr_ref[pl.ds(e, 1), :]
    g_f = mod_ref[0, 5:6, :]
    slot_col = lax.broadcasted_iota(jnp.int32, (MOE_ROWS, 1), 0).astype(F32)
    slot_row = lax.broadcasted_iota(jnp.int32, (1, MOE_ROWS), 1).astype(F32)
    n_steps = (cnt_ref[i * E + e] + MOE_ROWS - 1) // MOE_ROWS

    def body(c, carry):
        base = (c * MOE_ROWS).astype(F32)
        gather = jnp.where(rank_row == base + slot_col, 1.0, 0.0).astype(BF16)
        xg = _mm(gather, h_ref[...]).astype(BF16)
        gu = _mm(xg, wgu_ref[0])
        act = (_silu(gu[:, :F]) * gu[:, F:]).astype(BF16)
        y = (_mm(act, wd_ref[0]) * g_f).astype(BF16)
        scatter = jnp.where(rank_col == base + slot_row, gate_col, 0.0).astype(BF16)
        o_ref[...] += _mm(scatter, y)
        return carry

    lax.fori_loop(0, n_steps, body, 0)

    @pl.when(e == E - 1)
    def _():
        x = o_ref[...]
        o_ref[...] = x * lax.rsqrt(jnp.mean(x * x, axis=-1, keepdims=True) + NORM_EPS) * fn_ref[...]


def _moe(x1, h, comb, rank_c, rank_r, counts, mod, w_gu, w_d, final_g, S, tm):
    T, D = x1.shape
    E = w_gu.shape[0]
    tps = S // tm
    grid_spec = pltpu.PrefetchScalarGridSpec(
        num_scalar_prefetch=1,
        grid=(T // tm, E),
        in_specs=[pl.BlockSpec((tm, D), lambda i, e, c: (i, 0)),
                  pl.BlockSpec((tm, D), lambda i, e, c: (i, 0)),
                  pl.BlockSpec((tm, LANES), lambda i, e, c: (i, 0)),
                  pl.BlockSpec((tm, LANES), lambda i, e, c: (i, 0)),
                  pl.BlockSpec((8, tm), lambda i, e, c: (0, i)),
                  pl.BlockSpec((1, 6, D), lambda i, e, c: (i // tps, 0, 0)),
                  pl.BlockSpec((1,) + w_gu.shape[1:], lambda i, e, c: (e, 0, 0)),
                  pl.BlockSpec((1,) + w_d.shape[1:], lambda i, e, c: (e, 0, 0)),
                  pl.BlockSpec((1, D), lambda i, e, c: (0, 0))],
        out_specs=pl.BlockSpec((tm, D), lambda i, e, c: (i, 0)),
    )
    return pl.pallas_call(
        _moe_kernel,
        grid_spec=grid_spec,
        out_shape=jax.ShapeDtypeStruct((T, D), F32),
        compiler_params=_params("arbitrary", "arbitrary"),
        name="moe_ffn",
    )(counts, x1, h, comb, rank_c, rank_r, mod, w_gu, w_d, final_g)


def _pair_order(n_heads, n_kv):
    group = n_heads // n_kv
    order = []
    for j in range(n_kv // 2):
        for i in range(group):
            order += [(2 * j) * group + i, (2 * j + 1) * group + i]
    return np.asarray(order)


def _head_cols(order):
    return (order[:, None] * HEAD_DIM + np.arange(HEAD_DIM)[None, :]).reshape(-1)


def kernel(x, c, positions, e_ada_w, e_ada_b, e_norm_mix, e_norm_ffn, e_w_in, e_forget_b, e_sinks, e_w_out, e_ffn_gate, e_ffn_up, e_ffn_down, o_ada_w, o_ada_b, o_norm_mix, o_norm_ffn, o_w_in, o_w_out, o_router_w, o_router_b, o_exp_gate, o_exp_up, o_exp_down, final_norm):
    B, S, D = x.shape
    T = B * S
    scale = HEAD_DIM ** -0.5
    x2 = x.reshape(T, D)
    cos_t, sin_t = _rope_tables(positions)

    li = 0
    mod = _ada(c, e_ada_w[li], e_ada_b[li])
    w = e_w_in[li]
    nf = FOX_HEADS * HEAD_DIM
    c_fq, c_fk, c_fv, c_fg = 0, nf, 2 * nf, 3 * nf
    c_sq = c_fg + FOX_HEADS
    c_sk = c_sq + SWA_HEADS * HEAD_DIM
    c_sv = c_sk + SWA_KV_HEADS * HEAD_DIM
    swa_order = _pair_order(SWA_HEADS, SWA_KV_HEADS)
    swa_cols = _head_cols(swa_order)
    w_main = jnp.concatenate([w[:, c_fq:c_fk] * scale, w[:, c_fk:c_fg],
                              w[:, c_sq:c_sk][:, swa_cols] * scale, w[:, c_sk:]], axis=1).astype(BF16)
    w_fg = jnp.pad(w[:, c_fg:c_sq], ((0, 0), (0, LANES - FOX_HEADS))).astype(BF16)
    w_fgt = jnp.pad(w[:, c_fg:c_sq].T, ((0, ROWS16 - FOX_HEADS), (0, 0))).astype(BF16)
    fb_col = jnp.pad(e_forget_b[li], (0, LANES - FOX_HEADS))[None, :]
    fb_row = jnp.pad(e_forget_b[li], (0, ROWS16 - FOX_HEADS))[:, None]
    fq, fk, fv, cum_c, cum_r, sq, sk, sv = _even_proj(
        x2, e_norm_mix[li][None, :], mod, w_main, w_fg, w_fgt, fb_col, fb_row, cos_t, sin_t, S)
    o_fox = _fox_attention(fq, fk, fv, cum_c, cum_r, S)
    o_swa = _swa_attention(sq, sk, sv, e_sinks[li][swa_order], S)
    wo = e_w_out[li]
    x1, h = _post_attn(x2, mod, e_norm_ffn[li][None, :], [o_fox, o_swa],
                       [wo[:nf].astype(BF16), wo[nf:][swa_cols].astype(BF16)], S, min(512, S))
    x2 = _ffn(x1, h, mod, e_ffn_gate[li].astype(BF16), e_ffn_up[li].astype(BF16), e_ffn_down[li].astype(BF16), S)

    mod = _ada(c, o_ada_w[li], o_ada_b[li])
    w = o_w_in[li]
    c_q = 0
    c_k = DSA_HEADS * HEAD_DIM
    c_v = c_k + DSA_KV_HEADS * HEAD_DIM
    c_iq = c_v + DSA_KV_HEADS * HEAD_DIM
    c_ik = c_iq + IDX_HEADS * HEAD_DIM
    c_iw = c_ik + HEAD_DIM
    dsa_cols = _head_cols(_pair_order(DSA_HEADS, DSA_KV_HEADS))
    w_main = jnp.concatenate([w[:, c_q:c_k][:, dsa_cols] * scale, w[:, c_k:c_ik],
                              w[:, c_ik:c_iw], w[:, c_ik:c_iw]], axis=1).astype(BF16)
    w_iw = jnp.pad(w[:, c_iw:], ((0, 0), (0, LANES - IDX_HEADS))).astype(BF16)
    q, k, v, iq, ik, iw = _odd_proj(x2, o_norm_mix[li][None, :], mod, w_main, w_iw, cos_t, sin_t, S)
    o_dsa = _dsa_attention(q, k, v, iq, ik, iw, S)
    tm_moe = min(1024, S)
    x1, h, comb, rank_c, rank_r, counts = _post_attn(
        x2, mod, o_norm_ffn[li][None, :], [o_dsa], [o_w_out[li][dsa_cols].astype(BF16)], S, tm_moe,
        router=(o_router_w[li].T, o_router_b[li]))
    w_gu = jnp.concatenate([o_exp_gate[li], o_exp_up[li]], axis=2).astype(BF16)
    out = _moe(x1, h, comb, rank_c, rank_r, counts[:, 0, :N_EXPERTS].reshape(-1), mod,
               w_gu, o_exp_down[li].astype(BF16), final_norm[None, :], S, tm_moe)
    return out.reshape(B, S, D)
```

```python
import functools

import jax
import jax.numpy as jnp
import numpy as np
from jax import lax
from jax.experimental import pallas as pl
from jax.experimental.pallas import tpu as pltpu

HEAD_DIM = 64
LANES = 128
FOX_HEADS = 8
SWA_HEADS = 8
SWA_KV_HEADS = 2
SWA_WINDOW = 128
DSA_HEADS = 16
DSA_KV_HEADS = 4
IDX_HEADS = 8
DSA_TOPK_MAX = 256
N_EXPERTS = 8
ROPE_THETA = 10000.0
NORM_EPS = 1e-6
NEG = -1e30
INT_MIN = -2 ** 31
LOG2E = 1.4426950408889634
VMEM_LIMIT = 56 * 1024 * 1024
ROWS16 = 16
ATTN_KEYS = 128

F32 = jnp.float32
BF16 = jnp.bfloat16


def _mm(a, b):
    return jnp.dot(a, b, preferred_element_type=F32)


def _nt(a, b):
    return lax.dot_general(a, b, (((1,), (1,)), ((), ())), preferred_element_type=F32)


def _params(*sem):
    return pltpu.CompilerParams(dimension_semantics=sem, vmem_limit_bytes=VMEM_LIMIT)


def _norm_mod(x, g, scale, shift):
    y = x * lax.rsqrt(jnp.mean(x * x, axis=-1, keepdims=True) + NORM_EPS)
    return (y * g) * (1.0 + scale) + shift


def _log_sigmoid(z):
    return jnp.minimum(z, 0.0) - jnp.log1p(jnp.exp(-jnp.abs(z)))


def _silu(z):
    return z * (1.0 / (1.0 + jnp.exp(-z)))


def _split3(v):
    hi = v.astype(BF16)
    r1 = v - hi.astype(F32)
    mid = r1.astype(BF16)
    lo = (r1 - mid.astype(F32)).astype(BF16)
    return hi, mid, lo


def _pair_masks():
    lane = lax.broadcasted_iota(jnp.int32, (1, LANES), 1)
    return lane < HEAD_DIM


def _split_pair(q2, lo):
    zero = jnp.zeros_like(q2)
    return jnp.where(lo, q2, zero), jnp.where(lo, zero, q2)


def _rope128(t, cos, sin_signed):
    lane = lax.broadcasted_iota(jnp.int32, t.shape, 1)
    first_half = (lane & (HEAD_DIM - 1)) < (HEAD_DIM // 2)
    partner = jnp.where(first_half, pltpu.roll(t, LANES - HEAD_DIM // 2, 1), pltpu.roll(t, HEAD_DIM // 2, 1))
    return t * cos + partner * sin_signed


def _store_heads(out_ref, g, t, split):
    if split:
        a, b = _split_pair(t, _pair_masks())
        out_ref[:, 2 * g * LANES:(2 * g + 1) * LANES] = a
        out_ref[:, (2 * g + 1) * LANES:(2 * g + 2) * LANES] = b
    else:
        out_ref[:, g * LANES:(g + 1) * LANES] = t


def _project(hb, w_ref, lo_col, hi_col, out_ref, rope=None, split=False):
    t = _mm(hb, w_ref[:, lo_col:hi_col])
    for g in range((hi_col - lo_col) // LANES):
        tg = t[:, g * LANES:(g + 1) * LANES]
        if rope is not None:
            tg = _rope128(tg, *rope)
        _store_heads(out_ref, g, tg.astype(BF16), split)


def _store_vt(vt_ref, vt):
    tm = vt.shape[1]
    ones = jnp.ones((HEAD_DIM, tm), BF16)
    for n in range(vt.shape[0] // LANES):
        v = vt[n * LANES:(n + 1) * LANES].astype(BF16)
        base = 2 * n * LANES
        vt_ref[base:base + HEAD_DIM] = v[:HEAD_DIM]
        vt_ref[base + HEAD_DIM:base + LANES] = ones
        vt_ref[base + LANES:base + LANES + HEAD_DIM] = ones
        vt_ref[base + LANES + HEAD_DIM:base + 2 * LANES] = v[HEAD_DIM:]


def _rope_table_kernel(pos_ref, inv_ref, sign_ref, cos_ref, sin_ref):
    ang = pos_ref[...].astype(F32) * inv_ref[...]
    cos_ref[...] = jnp.cos(ang)
    sin_ref[...] = jnp.sin(ang) * sign_ref[...]


def _rope_tables(positions):
    T = positions.size
    tm = min(1024, T)
    half = HEAD_DIM // 2
    inv_freq = ROPE_THETA ** (-jnp.arange(half, dtype=F32) / half)
    inv128 = jnp.tile(inv_freq, LANES // half)[None, :]
    sign128 = jnp.tile(jnp.concatenate([-jnp.ones((half,), F32), jnp.ones((half,), F32)]), LANES // HEAD_DIM)[None, :]
    return pl.pallas_call(
        _rope_table_kernel,
        grid=(T // tm,),
        in_specs=[pl.BlockSpec((tm, 1), lambda i: (i, 0)),
                  pl.BlockSpec((1, LANES), lambda i: (0, 0)),
                  pl.BlockSpec((1, LANES), lambda i: (0, 0))],
        out_specs=[pl.BlockSpec((tm, LANES), lambda i: (i, 0))] * 2,
        out_shape=[jax.ShapeDtypeStruct((T, LANES), F32)] * 2,
        compiler_params=_params("arbitrary"),
        name="rope_tables",
    )(positions.reshape(T, 1), inv128, sign128)


def _ada_kernel(c_ref, w_ref, b_ref, o_ref):
    a = _silu(c_ref[...])
    o_ref[...] = jnp.dot(a, w_ref[...], preferred_element_type=F32, precision=lax.Precision.HIGHEST) + b_ref[...]


def _ada(c, w, b):
    B, D = c.shape
    N = w.shape[1]
    tn = N // 6
    mod = pl.pallas_call(
        _ada_kernel,
        grid=(N // tn,),
        in_specs=[pl.BlockSpec((B, D), lambda j: (0, 0)),
                  pl.BlockSpec((D, tn), lambda j: (0, j)),
                  pl.BlockSpec((1, tn), lambda j: (0, j))],
        out_specs=pl.BlockSpec((B, tn), lambda j: (0, j)),
        out_shape=jax.ShapeDtypeStruct((B, N), F32),
        compiler_params=_params("arbitrary"),
        name="ada_mod",
    )(c, w, b[None, :])
    return mod.reshape(B, 6, D)


E_FQ, E_FK, E_SQ, E_SK, E_END = 0, 512, 1024, 1536, 1664
E_FVT, E_SVT, E_VT_END = 0, 512, 640


def _even_proj_kernel(x_ref, g_ref, mod_ref, w_ref, wvt_ref, wfg_ref, wfgt_ref, fbc_ref, fbr_ref, cos_ref, sin_ref,
                      fq_ref, fk_ref, fvt_ref, cc_ref, cr_ref, sq_ref, sk_ref, svt_ref,
                      carc_ref, carr_ref, *, tiles_per_seq):
    i = pl.program_id(0)
    tm = x_ref.shape[0]

    @pl.when(i % tiles_per_seq == 0)
    def _():
        carc_ref[...] = jnp.zeros_like(carc_ref)
        carr_ref[...] = jnp.zeros_like(carr_ref)

    hb = _norm_mod(x_ref[...], g_ref[...], mod_ref[0, 1:2, :], mod_ref[0, 0:1, :]).astype(BF16)
    rope = (cos_ref[...], sin_ref[...])
    _project(hb, w_ref, E_FQ, E_FK, fq_ref)
    _project(hb, w_ref, E_FK, E_SQ, fk_ref, split=True)
    _project(hb, w_ref, E_SQ, E_SK, sq_ref, rope=rope)
    _project(hb, w_ref, E_SK, E_END, sk_ref, rope=rope)
    _store_vt(fvt_ref, _nt(wvt_ref[E_FVT:E_SVT, :], hb))
    svt_ref[...] = _nt(wvt_ref[E_SVT:E_VT_END, :], hb).astype(BF16)

    row = lax.broadcasted_iota(jnp.int32, (tm, tm), 0)
    col = lax.broadcasted_iota(jnp.int32, (tm, tm), 1)
    lower = jnp.where(col <= row, 1.0, 0.0).astype(BF16)
    upper = jnp.where(row <= col, 1.0, 0.0).astype(BF16)

    lf_c = _log_sigmoid(_mm(hb, wfg_ref[...]) + fbc_ref[...])
    h1, h2, h3 = _split3(lf_c)
    cum_c = _mm(lower, h1) + _mm(lower, h2) + _mm(lower, h3) + carc_ref[...]
    cc_ref[...] = cum_c * LOG2E
    carc_ref[...] = cum_c[tm - 1:tm, :]

    lf_r = _log_sigmoid(_nt(wfgt_ref[...], hb) + fbr_ref[...])
    r1, r2, r3 = _split3(lf_r)
    cum_r = _mm(r1, upper) + _mm(r2, upper) + _mm(r3, upper) + carr_ref[...]
    cr_ref[0] = cum_r * LOG2E
    carr_ref[...] = cum_r[:, tm - 1:tm]


def _even_proj(x2, g, mod, w_main, w_vt, w_fg, w_fgt, fb_col, fb_row, cos_t, sin_t, S):
    T, D = x2.shape
    B = T // S
    tm = min(512, S)
    tps = S // tm
    nfv = 2 * LANES * (FOX_HEADS // 2)
    tok = lambda w: pl.BlockSpec((tm, w), lambda i: (i, 0))
    tok_t = lambda w: pl.BlockSpec((w, tm), lambda i: (0, i))
    const = lambda a: pl.BlockSpec(a.shape, lambda i: (0,) * a.ndim)
    return pl.pallas_call(
        functools.partial(_even_proj_kernel, tiles_per_seq=tps),
        grid=(T // tm,),
        in_specs=[tok(D), const(g), pl.BlockSpec((1, 6, D), lambda i: (i // tps, 0, 0)),
                  const(w_main), const(w_vt), const(w_fg), const(w_fgt), const(fb_col), const(fb_row),
                  tok(LANES), tok(LANES)],
        out_specs=[tok(512), tok(1024), tok_t(nfv), tok(LANES),
                   pl.BlockSpec((1, ROWS16, tm), lambda i: (i // tps, 0, i % tps)),
                   tok(512), tok(LANES), tok_t(LANES)],
        out_shape=[jax.ShapeDtypeStruct((T, 512), BF16), jax.ShapeDtypeStruct((T, 1024), BF16),
                   jax.ShapeDtypeStruct((nfv, T), BF16),
                   jax.ShapeDtypeStruct((T, LANES), F32), jax.ShapeDtypeStruct((B, ROWS16, S), F32),
                   jax.ShapeDtypeStruct((T, 512), BF16), jax.ShapeDtypeStruct((T, LANES), BF16),
                   jax.ShapeDtypeStruct((LANES, T), BF16)],
        scratch_shapes=[pltpu.VMEM((1, LANES), F32), pltpu.VMEM((ROWS16, 1), F32)],
        compiler_params=_params("arbitrary"),
        name="even_proj",
    )(x2, g, mod, w_main, w_vt, w_fg, w_fgt, fb_col, fb_row, cos_t, sin_t)


def _head_step(s_t, vt_h, m_ref, l_ref, head, cq=None):
    m_old = m_ref[head]
    smax = jnp.max(s_t, axis=0, keepdims=True)
    if cq is not None:
        smax = smax + cq
    m_new = jnp.maximum(m_old, smax)
    m_ref[head] = m_new
    shift = m_new if cq is None else m_new - cq
    alpha = jnp.exp2(m_old - m_new)
    pv = _mm(vt_h, jnp.exp2(s_t - shift).astype(BF16))
    sum_row = HEAD_DIM if head % 2 == 0 else 0
    l_ref[head] = alpha * l_ref[head] + pv[sum_row:sum_row + 1]
    return alpha, pv


def _loop_key_blocks(n_blocks, steps_per_block, step):
    big = 2 * steps_per_block

    def body2(i, carry):
        for u in range(big):
            step(i * big + u)
        return carry

    def body1(i, carry):
        for u in range(steps_per_block):
            step((n_blocks // 2) * big + u)
        return carry

    lax.fori_loop(0, n_blocks // 2, body2, 0)
    lax.fori_loop(0, n_blocks % 2, body1, 0)


def _pair_rows(a, b, tq):
    return jnp.concatenate([jnp.broadcast_to(a, (HEAD_DIM, tq)), jnp.broadcast_to(b, (HEAD_DIM, tq))], axis=0)


def _pair_acc(acc_ref, p, alpha_a, pv_a, alpha_b, pv_b):
    tq = pv_a.shape[1]
    pv = jnp.concatenate([pv_a[:HEAD_DIM], pv_b[HEAD_DIM:]], axis=0)
    acc_ref[p] = _pair_rows(alpha_a, alpha_b, tq) * acc_ref[p] + pv


def _vt_blocks(vt_ref, blk, rows):
    base = 2 * blk * LANES
    return vt_ref[base:base + LANES, rows], vt_ref[base + LANES:base + 2 * LANES, rows]


def _init_state(m_ref, l_ref, acc_ref):
    m_ref[...] = jnp.full_like(m_ref, NEG)
    l_ref[...] = jnp.zeros_like(l_ref)
    acc_ref[...] = jnp.zeros_like(acc_ref)


def _finish_pair(l_ref, acc_ref, p):
    tq = acc_ref.shape[2]
    o_t = acc_ref[p] / _pair_rows(l_ref[2 * p], l_ref[2 * p + 1], tq)
    return jnp.transpose(o_t).astype(BF16)


def _attn_scratch(n_heads, tq):
    return [pltpu.VMEM((n_heads, 1, tq), F32), pltpu.VMEM((n_heads, 1, tq), F32),
            pltpu.VMEM((n_heads // 2, LANES, tq), F32)]


def _fox_kernel(q_ref, k_ref, vt_ref, cc_ref, cr_ref, o_ref, m_ref, l_ref, acc_ref, *, tk):
    tq = q_ref.shape[0]
    n_pairs = q_ref.shape[1] // LANES
    q0 = pl.program_id(1) * tq
    n_full = q0 // tk
    qpos = q0 + lax.broadcasted_iota(jnp.int32, (1, tq), 1)
    _init_state(m_ref, l_ref, acc_ref)

    def step(j, masked):
        rows = pl.ds(pl.multiple_of(j * tk, tk), tk)
        if masked:
            causal = (j * tk + lax.broadcasted_iota(jnp.int32, (tk, 1), 0)) <= qpos
        for p in range(n_pairs):
            q2 = q_ref[:, p * LANES:(p + 1) * LANES]
            vts = _vt_blocks(vt_ref, p, rows)
            upd = []
            for half, h in enumerate((2 * p, 2 * p + 1)):
                s_t = _nt(k_ref[rows, h * LANES:(h + 1) * LANES], q2) - cc_ref[rows, h:h + 1]
                if masked:
                    s_t = jnp.where(causal, s_t, NEG)
                upd += _head_step(s_t, vts[half], m_ref, l_ref, h, cr_ref[0, h:h + 1, :])
            _pair_acc(acc_ref, p, *upd)

    _loop_key_blocks(q0 // tq, tq // tk, functools.partial(step, masked=False))
    for d in range(tq // tk):
        step(n_full + d, True)
    for p in range(n_pairs):
        o_ref[:, p * LANES:(p + 1) * LANES] = _finish_pair(l_ref, acc_ref, p)


def _fox_attention(fq, fk, fvt, cum_c, cum_r, S):
    T, W = fq.shape
    B = T // S
    tq = min(256, S)
    nq = S // tq
    return pl.pallas_call(
        functools.partial(_fox_kernel, tk=min(ATTN_KEYS, S)),
        grid=(B, nq),
        in_specs=[pl.BlockSpec((tq, W), lambda b, i: (b * nq + i, 0)),
                  pl.BlockSpec((S, fk.shape[1]), lambda b, i: (b, 0)),
                  pl.BlockSpec((fvt.shape[0], S), lambda b, i: (0, b)),
                  pl.BlockSpec((S, LANES), lambda b, i: (b, 0)),
                  pl.BlockSpec((1, ROWS16, tq), lambda b, i: (b, 0, i))],
        out_specs=pl.BlockSpec((tq, W), lambda b, i: (b * nq + i, 0)),
        out_shape=jax.ShapeDtypeStruct((T, W), BF16),
        scratch_shapes=_attn_scratch(W // HEAD_DIM, tq),
        compiler_params=_params("arbitrary", "arbitrary"),
        name="fox_attention",
    )(fq, fk, fvt, cum_c, cum_r)


def _swa_kernel(sink_ref, q_ref, k_ref, vt_ref, o_ref):
    tq = q_ref.shape[0]
    S = k_ref.shape[0]
    span = min(tq + SWA_WINDOW, S)
    q0 = pl.program_id(1) * tq
    start = pl.multiple_of(jnp.maximum(q0 - SWA_WINDOW, 0), SWA_WINDOW)
    k2 = k_ref[pl.ds(start, span), :]
    vt2 = vt_ref[:, pl.ds(start, span)]
    lo = _pair_masks()
    qpos = q0 + lax.broadcasted_iota(jnp.int32, (1, tq), 1)
    kpos = start + lax.broadcasted_iota(jnp.int32, (span, 1), 0)
    valid = (kpos <= qpos) & (qpos - kpos < SWA_WINDOW)
    for p in range(q_ref.shape[1] // LANES):
        cols = slice(p * LANES, (p + 1) * LANES)
        outs = []
        for half, qh in enumerate(_split_pair(q_ref[:, cols], lo)):
            sink = sink_ref[2 * p + half] * LOG2E
            s_t = jnp.where(valid, _nt(k2, qh), NEG)
            m = jnp.maximum(jnp.max(s_t, axis=0, keepdims=True), sink)
            e_t = jnp.exp2(s_t - m)
            den = jnp.sum(e_t, axis=0, keepdims=True) + jnp.exp2(sink - m)
            outs.append(_mm(vt2, e_t.astype(BF16)) / den)
        o_t = jnp.concatenate([outs[0][:HEAD_DIM], outs[1][HEAD_DIM:]], axis=0)
        o_ref[:, cols] = jnp.transpose(o_t).astype(BF16)


def _swa_attention(sq, sk, svt, sinks, S):
    T, W = sq.shape
    B = T // S
    tq = min(256, S)
    nq = S // tq
    return pl.pallas_call(
        _swa_kernel,
        grid=(B, nq),
        in_specs=[pl.BlockSpec(memory_space=pltpu.SMEM),
                  pl.BlockSpec((tq, W), lambda b, i: (b * nq + i, 0)),
                  pl.BlockSpec((S, LANES), lambda b, i: (b, 0)),
                  pl.BlockSpec((LANES, S), lambda b, i: (0, b))],
        out_specs=pl.BlockSpec((tq, W), lambda b, i: (b * nq + i, 0)),
        out_shape=jax.ShapeDtypeStruct((T, W), BF16),
        compiler_params=_params("arbitrary", "arbitrary"),
        name="swa_attention",
    )(sinks, sq, sk, svt)


RANK_BLOCK = 256


def _post_attn_kernel(*refs, n_in, route):
    x_ref, mod_ref, g_ref = refs[0:3]
    o_refs = refs[3:3 + n_in]
    w_refs = refs[3 + n_in:3 + 2 * n_in]
    pos = 3 + 2 * n_in
    if route:
        rw_ref, rb_ref = refs[pos:pos + 2]
        pos += 2
    x1_ref, h_ref = refs[pos:pos + 2]
    pos += 2
    mix = _mm(o_refs[0][...], w_refs[0][...])
    for o_ref, w_ref in zip(o_refs[1:], w_refs[1:]):
        mix = mix + _mm(o_ref[...], w_ref[...])
    x1 = x_ref[...] + mod_ref[0, 2:3, :] * mix
    x1_ref[...] = x1
    h = _norm_mod(x1, g_ref[...], mod_ref[0, 4:5, :], mod_ref[0, 3:4, :])
    h_ref[...] = h.astype(BF16)
    if not route:
        return
    comb_ref, rankc_ref, rankr_ref, cnt_ref = refs[pos:pos + 4]
    tm = h.shape[0]
    E = rw_ref.shape[0]
    lane = lax.broadcasted_iota(jnp.int32, (tm, LANES), 1).astype(F32)
    logits = jnp.full((tm, LANES), -jnp.inf, F32)
    for e in range(E):
        le = jnp.sum(h * rw_ref[e:e + 1, :], axis=1, keepdims=True) + rb_ref[e]
        logits = jnp.where(lane == float(e), le, logits)
    top1 = jnp.max(logits, axis=1, keepdims=True)
    idx1 = jnp.min(jnp.where(logits == top1, lane, float(LANES)), axis=1, keepdims=True)
    rest = jnp.where(lane == idx1, -jnp.inf, logits)
    top2 = jnp.max(rest, axis=1, keepdims=True)
    idx2 = jnp.min(jnp.where(rest == top2, lane, float(LANES)), axis=1, keepdims=True)
    e2 = jnp.exp(top2 - top1)
    g1 = 1.0 / (1.0 + e2)
    g2 = e2 / (1.0 + e2)
    is1 = lane == idx1
    is2 = lane == idx2
    comb_ref[...] = jnp.where(is1, g1, 0.0) + jnp.where(is2, g2, 0.0)
    sel = jnp.where(is1 | is2, 1.0, 0.0)
    rb = min(RANK_BLOCK, tm)
    row = lax.broadcasted_iota(jnp.int32, (rb, rb), 0)
    col = lax.broadcasted_iota(jnp.int32, (rb, rb), 1)
    strict_lower = jnp.where(col < row, 1.0, 0.0).astype(BF16)
    seen = jnp.zeros((1, LANES), F32)
    for blk in range(tm // rb):
        sel_b = sel[blk * rb:(blk + 1) * rb, :]
        rank = _mm(strict_lower, sel_b.astype(BF16)) + seen
        rankc_ref[blk * rb:(blk + 1) * rb, :] = jnp.where(sel_b > 0.0, rank, -1.0)
        seen = seen + jnp.sum(sel_b, axis=0, keepdims=True)
    rankr_ref[...] = jnp.transpose(rankc_ref[...])[0:rankr_ref.shape[0], :]
    cnt_ref[0] = seen.astype(jnp.int32)


def _post_attn(x2, mod, g, os_, ws, S, tm, router=None):
    T, D = x2.shape
    tps = S // tm
    n_in = len(os_)
    tok = lambda w: pl.BlockSpec((tm, w), lambda i: (i, 0))
    const = lambda a: pl.BlockSpec(a.shape, lambda i: (0,) * a.ndim)
    in_specs = [tok(D), pl.BlockSpec((1, 6, D), lambda i: (i // tps, 0, 0)), const(g)]
    in_specs += [tok(o.shape[1]) for o in os_] + [const(w) for w in ws]
    args = [x2, mod, g, *os_, *ws]
    out_specs = [tok(D), tok(D)]
    out_shape = [jax.ShapeDtypeStruct((T, D), F32), jax.ShapeDtypeStruct((T, D), BF16)]
    if router is not None:
        rw_t, rb = router
        in_specs += [const(rw_t), pl.BlockSpec(memory_space=pltpu.SMEM)]
        args += [rw_t, rb]
        nT = T // tm
        out_specs += [tok(LANES), tok(LANES), pl.BlockSpec((8, tm), lambda i: (0, i)),
                      pl.BlockSpec((1, 1, LANES), lambda i: (i, 0, 0))]
        out_shape += [jax.ShapeDtypeStruct((T, LANES), F32), jax.ShapeDtypeStruct((T, LANES), F32),
                      jax.ShapeDtypeStruct((8, T), F32), jax.ShapeDtypeStruct((nT, 1, LANES), jnp.int32)]
    return pl.pallas_call(
        functools.partial(_post_attn_kernel, n_in=n_in, route=router is not None),
        grid=(T // tm,),
        in_specs=in_specs,
        out_specs=out_specs,
        out_shape=out_shape,
        compiler_params=_params("arbitrary"),
        name="post_attn_route" if router is not None else "post_attn",
    )(*args)


def _ffn_kernel(x1_ref, h_ref, mod_ref, wg_ref, wu_ref, wd_ref, o_ref, acc_ref):
    k = pl.program_id(1)
    h = h_ref[...]
    act = (_silu(_mm(h, wg_ref[...])) * _mm(h, wu_ref[...])).astype(BF16)
    part = _mm(act, wd_ref[...])

    @pl.when(k == 0)
    def _():
        acc_ref[...] = part

    @pl.when(k > 0)
    def _():
        acc_ref[...] += part

    @pl.when(k == pl.num_programs(1) - 1)
    def _():
        o_ref[...] = x1_ref[...] + mod_ref[0, 5:6, :] * acc_ref[...]


def _ffn(x1, h, mod, wg, wu, wd, S):
    T, D = x1.shape
    F = wg.shape[1]
    tm = min(512, S)
    tps = S // tm
    nf = 2
    tf = F // nf
    return pl.pallas_call(
        _ffn_kernel,
        grid=(T // tm, nf),
        in_specs=[pl.BlockSpec((tm, D), lambda i, k: (i, 0)),
                  pl.BlockSpec((tm, D), lambda i, k: (i, 0)),
                  pl.BlockSpec((1, 6, D), lambda i, k: (i // tps, 0, 0)),
                  pl.BlockSpec((D, tf), lambda i, k: (0, k)),
                  pl.BlockSpec((D, tf), lambda i, k: (0, k)),
                  pl.BlockSpec((tf, D), lambda i, k: (k, 0))],
        out_specs=pl.BlockSpec((tm, D), lambda i, k: (i, 0)),
        out_shape=jax.ShapeDtypeStruct((T, D), F32),
        scratch_shapes=[pltpu.VMEM((tm, D), F32)],
        compiler_params=_params("arbitrary", "arbitrary"),
        name="dense_ffn",
    )(x1, h, mod, wg, wu, wd)


O_Q, O_K, O_IQ, O_IK, O_END = 0, 1024, 1280, 1792, 2048
IW_SCALE = IDX_HEADS ** -0.5 * HEAD_DIM ** -0.5


def _odd_proj_kernel(x_ref, g_ref, mod_ref, w_ref, wvt_ref, wiwt_ref, cos_ref, sin_ref,
                     q_ref, k_ref, vt_ref, iq_ref, ik_ref, iwt_ref):
    hb = _norm_mod(x_ref[...], g_ref[...], mod_ref[0, 1:2, :], mod_ref[0, 0:1, :]).astype(BF16)
    rope = (cos_ref[...], sin_ref[...])
    _project(hb, w_ref, O_Q, O_K, q_ref, rope=rope)
    _project(hb, w_ref, O_K, O_IQ, k_ref, rope=rope, split=True)
    _project(hb, w_ref, O_IQ, O_IK, iq_ref, rope=rope)
    _project(hb, w_ref, O_IK, O_END, ik_ref, rope=rope)
    _store_vt(vt_ref, _nt(wvt_ref[...], hb))
    iwt_ref[0] = _nt(wiwt_ref[...], hb) * IW_SCALE


def _odd_proj(x2, g, mod, w_main, w_vt, w_iwt, cos_t, sin_t, S):
    T, D = x2.shape
    B = T // S
    tm = min(512, S)
    tps = S // tm
    tok = lambda w: pl.BlockSpec((tm, w), lambda i: (i, 0))
    const = lambda a: pl.BlockSpec(a.shape, lambda i: (0,) * a.ndim)
    nv = 2 * LANES * (DSA_KV_HEADS // 2)
    nk = 2 * (O_IQ - O_K)
    return pl.pallas_call(
        _odd_proj_kernel,
        grid=(T // tm,),
        in_specs=[tok(D), const(g), pl.BlockSpec((1, 6, D), lambda i: (i // tps, 0, 0)),
                  const(w_main), const(w_vt), const(w_iwt), tok(LANES), tok(LANES)],
        out_specs=[tok(O_K - O_Q), tok(nk), pl.BlockSpec((nv, tm), lambda i: (0, i)),
                   tok(O_IK - O_IQ), tok(O_END - O_IK),
                   pl.BlockSpec((1, ROWS16, tm), lambda i: (i // tps, 0, i % tps))],
        out_shape=[jax.ShapeDtypeStruct((T, O_K - O_Q), BF16), jax.ShapeDtypeStruct((T, nk), BF16),
                   jax.ShapeDtypeStruct((nv, T), BF16),
                   jax.ShapeDtypeStruct((T, O_IK - O_IQ), BF16), jax.ShapeDtypeStruct((T, O_END - O_IK), BF16),
                   jax.ShapeDtypeStruct((B, ROWS16, S), F32)],
        compiler_params=_params("arbitrary"),
        name="odd_proj",
    )(x2, g, mod, w_main, w_vt, w_iwt, cos_t, sin_t)


def _dsa_kernel(q_ref, k_ref, vt_ref, iq_ref, ik_ref, iwt_ref, o_ref,
                key_ref, bias_ref, m_ref, l_ref, acc_ref, *, tk, tka, topk):
    tq = q_ref.shape[0]
    n_pairs = q_ref.shape[1] // LANES
    q0 = pl.program_id(1) * tq
    nkv = (q0 + tq + tk - 1) // tk
    qpos = q0 + lax.broadcasted_iota(jnp.int32, (1, tq), 1)
    kf = float(topk)
    int_min = jnp.int32(INT_MIN)

    def chunk(j):
        return pl.ds(pl.multiple_of(j * tk, tk), tk)

    def score_body(j, carry):
        sc = jnp.zeros((tk, tq), F32)
        for hp in range(iq_ref.shape[1] // LANES):
            iq2 = iq_ref[:, hp * LANES:(hp + 1) * LANES]
            for half in range(2):
                h = 2 * hp + half
                logit = _nt(ik_ref[chunk(j), half * LANES:(half + 1) * LANES], iq2)
                sc = sc + iwt_ref[0, h:h + 1, :] * jnp.maximum(logit, 0.0)
        sc = jnp.where(sc == 0.0, 0.0, sc)
        bits = lax.bitcast_convert_type(sc, jnp.int32)
        key = jnp.where(bits < 0, bits ^ jnp.int32(0x7FFFFFFF), bits)
        causal = (j * tk + lax.broadcasted_iota(jnp.int32, (tk, 1), 0)) <= qpos
        key_ref[chunk(j), :] = jnp.where(causal, key, int_min)
        return carry

    lax.fori_loop(0, nkv, score_body, 0)

    def count(pred):
        def body(j, acc):
            ind = jnp.where(pred(key_ref[chunk(j), :]), 1.0, 0.0)
            return acc + jnp.sum(ind.reshape(tk // 8, 8, tq), axis=0)
        return jnp.sum(lax.fori_loop(0, nkv, body, jnp.zeros((8, tq), F32)), axis=0, keepdims=True)

    zero = jnp.zeros((1, tq), jnp.int32)
    thr = jnp.where(count(lambda kc: kc >= zero) >= kf, zero, int_min)

    def bit_body(it, thr):
        cand = thr + jnp.left_shift(jnp.int32(1), 30 - it)
        return jnp.where(count(lambda kc: kc >= cand) >= kf, cand, thr)

    thr = lax.fori_loop(0, 31, bit_body, thr)
    n_gt = count(lambda kc: kc > thr)
    n_ge = count(lambda kc: kc >= thr)
    need = kf - n_gt
    has_tie = jnp.max(jnp.where((n_ge > kf) & (thr != int_min), 1.0, 0.0)) > 0.0

    @pl.when(jnp.logical_not(has_tie))
    def _():
        def body(j, carry):
            kc = key_ref[chunk(j), :]
            bias_ref[chunk(j), :] = jnp.where((kc >= thr) & (kc > int_min), 0.0, NEG)
            return carry
        lax.fori_loop(0, nkv, body, 0)

    @pl.when(has_tie)
    def _():
        r = lax.broadcasted_iota(jnp.int32, (tk, tk), 0)
        c = lax.broadcasted_iota(jnp.int32, (tk, tk), 1)
        strict_lower = jnp.where(c < r, 1.0, 0.0).astype(BF16)

        def body(j, seen):
            kc = key_ref[chunk(j), :]
            eq = kc == thr
            eqf = jnp.where(eq, 1.0, 0.0)
            before = _mm(strict_lower, eqf.astype(BF16)) + seen
            sel = ((kc > thr) | (eq & (before < need))) & (kc > int_min)
            bias_ref[chunk(j), :] = jnp.where(sel, 0.0, NEG)
            return seen + jnp.sum(eqf, axis=0, keepdims=True)
        lax.fori_loop(0, nkv, body, jnp.zeros((1, tq), F32))

    pairs_per_block = n_pairs // (k_ref.shape[1] // (2 * LANES))
    _init_state(m_ref, l_ref, acc_ref)

    def attn_step(j):
        rows = pl.ds(pl.multiple_of(j * tka, tka), tka)
        bias = bias_ref[rows, :]
        for p in range(n_pairs):
            blk = p // pairs_per_block
            q2 = q_ref[:, p * LANES:(p + 1) * LANES]
            vts = _vt_blocks(vt_ref, blk, rows)
            upd = []
            for half in range(2):
                kh = 2 * blk + half
                s_t = _nt(k_ref[rows, kh * LANES:(kh + 1) * LANES], q2) + bias
                upd += _head_step(s_t, vts[half], m_ref, l_ref, 2 * p + half)
            _pair_acc(acc_ref, p, *upd)

    _loop_key_blocks((q0 + tq) // tq, tq // tka, attn_step)
    for p in range(n_pairs):
        o_ref[:, p * LANES:(p + 1) * LANES] = _finish_pair(l_ref, acc_ref, p)


def _dsa_attention(q, k, vt, iq, ik, iwt, S):
    T, W = q.shape
    B = T // S
    tq = min(256, S)
    tk = min(512, S)
    nq = S // tq
    topk = min(DSA_TOPK_MAX, S // 4)
    qtok = lambda w: pl.BlockSpec((tq, w), lambda b, i: (b * nq + i, 0))
    seq = lambda w: pl.BlockSpec((S, w), lambda b, i: (b, 0))
    return pl.pallas_call(
        functools.partial(_dsa_kernel, tk=tk, tka=min(ATTN_KEYS, S), topk=topk),
        grid=(B, nq),
        in_specs=[qtok(W), seq(k.shape[1]), pl.BlockSpec((vt.shape[0], S), lambda b, i: (0, b)),
                  qtok(iq.shape[1]), seq(ik.shape[1]), pl.BlockSpec((1, ROWS16, tq), lambda b, i: (b, 0, i))],
        out_specs=qtok(W),
        out_shape=jax.ShapeDtypeStruct((T, W), BF16),
        scratch_shapes=[pltpu.VMEM((S, tq), jnp.int32), pltpu.VMEM((S, tq), F32)] + _attn_scratch(W // HEAD_DIM, tq),
        compiler_params=_params("arbitrary", "arbitrary"),
        name="dsa_attention",
    )(q, k, vt, iq, ik, iwt)


MOE_ROWS = 128


def _moe_kernel(cnt_ref, x1_ref, h_ref, comb_ref, rankc_ref, rankr_ref, mod_ref, wgu_ref, wd_ref, fn_ref, o_ref):
    i = pl.program_id(0)
    e = pl.program_id(1)
    E = pl.num_programs(1)
    tm = h_ref.shape[0]
    F = wd_ref.shape[1]

    @pl.when(e == 0)
    def _():
        o_ref[...] = x1_ref[...]

    lane = lax.broadcasted_iota(jnp.int32, (tm, LANES), 1)
    mine = lane == e
    rank_col = jnp.sum(jnp.where(mine, rankc_ref[...], 0.0), axis=1, keepdims=True)
    gate_col = jnp.sum(jnp.where(mine, comb_ref[...], 0.0), axis=1, keepdims=True)
    rank_row = rankr_ref[pl.ds(e, 1), :]
    g_f = mod_ref[0, 5:6, :]
    slot_col = lax.broadcasted_iota(jnp.int32, (MOE_ROWS, 1), 0).astype(F32)
    slot_row = lax.broadcasted_iota(jnp.int32, (1, MOE_ROWS), 1).astype(F32)
    n_steps = (cnt_ref[i * E + e] + MOE_ROWS - 1) // MOE_ROWS

    def body(c, carry):
        base = (c * MOE_ROWS).astype(F32)
        gather = jnp.where(rank_row == base + slot_col, 1.0, 0.0).astype(BF16)
        xg = _mm(gather, h_ref[...]).astype(BF16)
        gu = _mm(xg, wgu_ref[0])
        act = (_silu(gu[:, :F]) * gu[:, F:]).astype(BF16)
        y = (_mm(act, wd_ref[0]) * g_f).astype(BF16)
        scatter = jnp.where(rank_col == base + slot_row, gate_col, 0.0).astype(BF16)
        o_ref[...] += _mm(scatter, y)
        return carry

    lax.fori_loop(0, n_steps, body, 0)

    @pl.when(e == E - 1)
    def _():
        x = o_ref[...]
        o_ref[...] = x * lax.rsqrt(jnp.mean(x * x, axis=-1, keepdims=True) + NORM_EPS) * fn_ref[...]


def _moe(x1, h, comb, rank_c, rank_r, counts, mod, w_gu, w_d, final_g, S, tm):
    T, D = x1.shape
    E = w_gu.shape[0]
    tps = S // tm
    grid_spec = pltpu.PrefetchScalarGridSpec(
        num_scalar_prefetch=1,
        grid=(T // tm, E),
        in_specs=[pl.BlockSpec((tm, D), lambda i, e, c: (i, 0)),
                  pl.BlockSpec((tm, D), lambda i, e, c: (i, 0)),
                  pl.BlockSpec((tm, LANES), lambda i, e, c: (i, 0)),
                  pl.BlockSpec((tm, LANES), lambda i, e, c: (i, 0)),
                  pl.BlockSpec((8, tm), lambda i, e, c: (0, i)),
                  pl.BlockSpec((1, 6, D), lambda i, e, c: (i // tps, 0, 0)),
                  pl.BlockSpec((1,) + w_gu.shape[1:], lambda i, e, c: (e, 0, 0)),
                  pl.BlockSpec((1,) + w_d.shape[1:], lambda i, e, c: (e, 0, 0)),
                  pl.BlockSpec((1, D), lambda i, e, c: (0, 0))],
        out_specs=pl.BlockSpec((tm, D), lambda i, e, c: (i, 0)),
    )
    return pl.pallas_call(
        _moe_kernel,
        grid_spec=grid_spec,
        out_shape=jax.ShapeDtypeStruct((T, D), F32),
        compiler_params=_params("arbitrary", "arbitrary"),
        name="moe_ffn",
    )(counts, x1, h, comb, rank_c, rank_r, mod, w_gu, w_d, final_g)


def _pair_order(n_heads, n_kv):
    group = n_heads // n_kv
    order = []
    for j in range(n_kv // 2):
        for i in range(group):
            order += [(2 * j) * group + i, (2 * j + 1) * group + i]
    return np.asarray(order)


def _head_cols(order):
    return (order[:, None] * HEAD_DIM + np.arange(HEAD_DIM)[None, :]).reshape(-1)


def _pad_rows(a, rows):
    return jnp.pad(a, ((0, rows - a.shape[0]), (0, 0)))


def kernel(x, c, positions, e_ada_w, e_ada_b, e_norm_mix, e_norm_ffn, e_w_in, e_forget_b, e_sinks, e_w_out, e_ffn_gate, e_ffn_up, e_ffn_down, o_ada_w, o_ada_b, o_norm_mix, o_norm_ffn, o_w_in, o_w_out, o_router_w, o_router_b, o_exp_gate, o_exp_up, o_exp_down, final_norm):
    B, S, D = x.shape
    T = B * S
    q_scale = HEAD_DIM ** -0.5 * LOG2E
    x2 = x.reshape(T, D)
    cos_t, sin_t = _rope_tables(positions)

    li = 0
    mod = _ada(c, e_ada_w[li], e_ada_b[li])
    w = e_w_in[li]
    nf = FOX_HEADS * HEAD_DIM
    c_fq, c_fk, c_fv, c_fg = 0, nf, 2 * nf, 3 * nf
    c_sq = c_fg + FOX_HEADS
    c_sk = c_sq + SWA_HEADS * HEAD_DIM
    c_sv = c_sk + SWA_KV_HEADS * HEAD_DIM
    swa_order = _pair_order(SWA_HEADS, SWA_KV_HEADS)
    swa_cols = _head_cols(swa_order)
    w_main = jnp.concatenate([w[:, c_fq:c_fk] * q_scale, w[:, c_fk:c_fv],
                              w[:, c_sq:c_sk][:, swa_cols] * q_scale, w[:, c_sk:c_sv]], axis=1).astype(BF16)
    w_vt = jnp.concatenate([w[:, c_fv:c_fg], w[:, c_sv:]], axis=1).T.astype(BF16)
    w_fg = jnp.pad(w[:, c_fg:c_sq], ((0, 0), (0, LANES - FOX_HEADS))).astype(BF16)
    w_fgt = _pad_rows(w[:, c_fg:c_sq].T, ROWS16).astype(BF16)
    fb_col = jnp.pad(e_forget_b[li], (0, LANES - FOX_HEADS))[None, :]
    fb_row = jnp.pad(e_forget_b[li], (0, ROWS16 - FOX_HEADS))[:, None]
    fq, fk, fvt, cum_c, cum_r, sq, sk, svt = _even_proj(
        x2, e_norm_mix[li][None, :], mod, w_main, w_vt, w_fg, w_fgt, fb_col, fb_row, cos_t, sin_t, S)
    o_fox = _fox_attention(fq, fk, fvt, cum_c, cum_r, S)
    o_swa = _swa_attention(sq, sk, svt, e_sinks[li][swa_order], S)
    wo = e_w_out[li]
    x1, h = _post_attn(x2, mod, e_norm_ffn[li][None, :], [o_fox, o_swa],
                       [wo[:nf].astype(BF16), wo[nf:][swa_cols].astype(BF16)], S, min(512, S))
    x2 = _ffn(x1, h, mod, e_ffn_gate[li].astype(BF16), e_ffn_up[li].astype(BF16), e_ffn_down[li].astype(BF16), S)

    mod = _ada(c, o_ada_w[li], o_ada_b[li])
    w = o_w_in[li]
    c_k = DSA_HEADS * HEAD_DIM
    c_v = c_k + DSA_KV_HEADS * HEAD_DIM
    c_iq = c_v + DSA_KV_HEADS * HEAD_DIM
    c_ik = c_iq + IDX_HEADS * HEAD_DIM
    c_iw = c_ik + HEAD_DIM
    dsa_cols = _head_cols(_pair_order(DSA_HEADS, DSA_KV_HEADS))
    w_ik = w[:, c_ik:c_iw]
    no_ik = jnp.zeros_like(w_ik)
    w_main = jnp.concatenate([w[:, :c_k][:, dsa_cols] * q_scale, w[:, c_k:c_v], w[:, c_iq:c_ik],
                              w_ik, no_ik, no_ik, w_ik], axis=1).astype(BF16)
    w_vt = w[:, c_v:c_iq].T.astype(BF16)
    w_iwt = _pad_rows(w[:, c_iw:].T, ROWS16).astype(BF16)
    q, k, vt, iq, ik, iwt = _odd_proj(x2, o_norm_mix[li][None, :], mod, w_main, w_vt, w_iwt, cos_t, sin_t, S)
    o_dsa = _dsa_attention(q, k, vt, iq, ik, iwt, S)
    tm_moe = min(1024, S)
    x1, h, comb, rank_c, rank_r, counts = _post_attn(
        x2, mod, o_norm_ffn[li][None, :], [o_dsa], [o_w_out[li][dsa_cols].astype(BF16)], S, tm_moe,
        router=(o_router_w[li].T, o_router_b[li]))
    w_gu = jnp.concatenate([o_exp_gate[li], o_exp_up[li]], axis=2).astype(BF16)
    out = _moe(x1, h, comb, rank_c, rank_r, counts[:, 0, :N_EXPERTS].reshape(-1), mod,
               w_gu, o_exp_down[li].astype(BF16), final_norm[None, :], S, tm_moe)
    return out.reshape(B, S, D)
```

```python
import functools

import jax
import jax.numpy as jnp
import numpy as np
from jax import lax
from jax.experimental import pallas as pl
from jax.experimental.pallas import tpu as pltpu

HEAD_DIM = 64
LANES = 128
FOX_HEADS = 8
SWA_HEADS = 8
SWA_KV_HEADS = 2
SWA_WINDOW = 128
DSA_HEADS = 16
DSA_KV_HEADS = 4
IDX_HEADS = 8
DSA_TOPK_MAX = 256
N_EXPERTS = 8
ROPE_THETA = 10000.0
NORM_EPS = 1e-6
NEG = -1e30
INT_MIN = -2 ** 31
LOG2E = 1.4426950408889634
VMEM_LIMIT = 56 * 1024 * 1024
ROWS16 = 16
ATTN_KEYS = 128

F32 = jnp.float32
BF16 = jnp.bfloat16


def _mm(a, b):
    return jnp.dot(a, b, preferred_element_type=F32)


def _nt(a, b):
    return lax.dot_general(a, b, (((1,), (1,)), ((), ())), preferred_element_type=F32)


def _params(*sem):
    return pltpu.CompilerParams(dimension_semantics=sem, vmem_limit_bytes=VMEM_LIMIT)


def _norm_mod(x, g, scale, shift):
    y = x * lax.rsqrt(jnp.mean(x * x, axis=-1, keepdims=True) + NORM_EPS)
    return (y * g) * (1.0 + scale) + shift


def _log_sigmoid(z):
    return jnp.minimum(z, 0.0) - jnp.log1p(jnp.exp(-jnp.abs(z)))


def _silu(z):
    return z * (1.0 / (1.0 + jnp.exp(-z)))


def _split3(v):
    hi = v.astype(BF16)
    r1 = v - hi.astype(F32)
    mid = r1.astype(BF16)
    lo = (r1 - mid.astype(F32)).astype(BF16)
    return hi, mid, lo


def _pair_masks():
    lane = lax.broadcasted_iota(jnp.int32, (1, LANES), 1)
    return lane < HEAD_DIM


def _split_pair(q2, lo):
    zero = jnp.zeros_like(q2)
    return jnp.where(lo, q2, zero), jnp.where(lo, zero, q2)


def _rope128(t, cos, sin_signed):
    lane = lax.broadcasted_iota(jnp.int32, t.shape, 1)
    first_half = (lane & (HEAD_DIM - 1)) < (HEAD_DIM // 2)
    partner = jnp.where(first_half, pltpu.roll(t, LANES - HEAD_DIM // 2, 1), pltpu.roll(t, HEAD_DIM // 2, 1))
    return t * cos + partner * sin_signed


def _store_heads(out_ref, g, t, split):
    if split:
        a, b = _split_pair(t, _pair_masks())
        out_ref[:, 2 * g * LANES:(2 * g + 1) * LANES] = a
        out_ref[:, (2 * g + 1) * LANES:(2 * g + 2) * LANES] = b
    else:
        out_ref[:, g * LANES:(g + 1) * LANES] = t


def _project(hb, w_ref, lo_col, hi_col, out_ref, rope=None, split=False):
    t = _mm(hb, w_ref[:, lo_col:hi_col])
    for g in range((hi_col - lo_col) // LANES):
        tg = t[:, g * LANES:(g + 1) * LANES]
        if rope is not None:
            tg = _rope128(tg, *rope)
        _store_heads(out_ref, g, tg.astype(BF16), split)


def _store_vt(vt_ref, vt):
    tm = vt.shape[1]
    ones = jnp.ones((HEAD_DIM, tm), BF16)
    for n in range(vt.shape[0] // LANES):
        v = vt[n * LANES:(n + 1) * LANES].astype(BF16)
        base = 2 * n * LANES
        vt_ref[base:base + HEAD_DIM] = v[:HEAD_DIM]
        vt_ref[base + HEAD_DIM:base + LANES] = ones
        vt_ref[base + LANES:base + LANES + HEAD_DIM] = ones
        vt_ref[base + LANES + HEAD_DIM:base + 2 * LANES] = v[HEAD_DIM:]


def _rope_table_kernel(pos_ref, inv_ref, sign_ref, cos_ref, sin_ref):
    ang = pos_ref[...].astype(F32) * inv_ref[...]
    cos_ref[...] = jnp.cos(ang)
    sin_ref[...] = jnp.sin(ang) * sign_ref[...]


def _rope_tables(positions):
    T = positions.size
    tm = min(1024, T)
    half = HEAD_DIM // 2
    inv_freq = ROPE_THETA ** (-jnp.arange(half, dtype=F32) / half)
    inv128 = jnp.tile(inv_freq, LANES // half)[None, :]
    sign128 = jnp.tile(jnp.concatenate([-jnp.ones((half,), F32), jnp.ones((half,), F32)]), LANES // HEAD_DIM)[None, :]
    return pl.pallas_call(
        _rope_table_kernel,
        grid=(T // tm,),
        in_specs=[pl.BlockSpec((tm, 1), lambda i: (i, 0)),
                  pl.BlockSpec((1, LANES), lambda i: (0, 0)),
                  pl.BlockSpec((1, LANES), lambda i: (0, 0))],
        out_specs=[pl.BlockSpec((tm, LANES), lambda i: (i, 0))] * 2,
        out_shape=[jax.ShapeDtypeStruct((T, LANES), F32)] * 2,
        compiler_params=_params("arbitrary"),
        name="rope_tables",
    )(positions.reshape(T, 1), inv128, sign128)


def _ada_kernel(c_ref, w_ref, b_ref, o_ref):
    a = _silu(c_ref[...])
    o_ref[...] = jnp.dot(a, w_ref[...], preferred_element_type=F32, precision=lax.Precision.HIGHEST) + b_ref[...]


def _ada(c, w, b):
    B, D = c.shape
    N = w.shape[1]
    tn = N // 6
    mod = pl.pallas_call(
        _ada_kernel,
        grid=(N // tn,),
        in_specs=[pl.BlockSpec((B, D), lambda j: (0, 0)),
                  pl.BlockSpec((D, tn), lambda j: (0, j)),
                  pl.BlockSpec((1, tn), lambda j: (0, j))],
        out_specs=pl.BlockSpec((B, tn), lambda j: (0, j)),
        out_shape=jax.ShapeDtypeStruct((B, N), F32),
        compiler_params=_params("arbitrary"),
        name="ada_mod",
    )(c, w, b[None, :])
    return mod.reshape(B, 6, D)


E_FQ, E_FK, E_SQ, E_SK, E_END = 0, 512, 1024, 1536, 1664
E_FVT, E_SVT, E_VT_END = 0, 512, 640


def _even_proj_kernel(x_ref, g_ref, mod_ref, w_ref, wvt_ref, wfg_ref, wfgt_ref, fbc_ref, fbr_ref, cos_ref, sin_ref,
                      fq_ref, fk_ref, fvt_ref, cc_ref, cr_ref, sq_ref, sk_ref, svt_ref,
                      carc_ref, carr_ref, *, tiles_per_seq):
    i = pl.program_id(0)
    tm = x_ref.shape[0]

    @pl.when(i % tiles_per_seq == 0)
    def _():
        carc_ref[...] = jnp.zeros_like(carc_ref)
        carr_ref[...] = jnp.zeros_like(carr_ref)

    hb = _norm_mod(x_ref[...], g_ref[...], mod_ref[0, 1:2, :], mod_ref[0, 0:1, :]).astype(BF16)
    rope = (cos_ref[...], sin_ref[...])
    _project(hb, w_ref, E_FQ, E_FK, fq_ref)
    _project(hb, w_ref, E_FK, E_SQ, fk_ref, split=True)
    _project(hb, w_ref, E_SQ, E_SK, sq_ref, rope=rope)
    _project(hb, w_ref, E_SK, E_END, sk_ref, rope=rope, split=True)
    _store_vt(fvt_ref, _nt(wvt_ref[E_FVT:E_SVT, :], hb))
    _store_vt(svt_ref, _nt(wvt_ref[E_SVT:E_VT_END, :], hb))

    row = lax.broadcasted_iota(jnp.int32, (tm, tm), 0)
    col = lax.broadcasted_iota(jnp.int32, (tm, tm), 1)
    lower = jnp.where(col <= row, 1.0, 0.0).astype(BF16)
    upper = jnp.where(row <= col, 1.0, 0.0).astype(BF16)

    lf_c = _log_sigmoid(_mm(hb, wfg_ref[...]) + fbc_ref[...])
    h1, h2, h3 = _split3(lf_c)
    cum_c = _mm(lower, h1) + _mm(lower, h2) + _mm(lower, h3) + carc_ref[...]
    cc_ref[...] = cum_c * LOG2E
    carc_ref[...] = cum_c[tm - 1:tm, :]

    lf_r = _log_sigmoid(_nt(wfgt_ref[...], hb) + fbr_ref[...])
    r1, r2, r3 = _split3(lf_r)
    cum_r = _mm(r1, upper) + _mm(r2, upper) + _mm(r3, upper) + carr_ref[...]
    cr_ref[0] = cum_r * LOG2E
    carr_ref[...] = cum_r[:, tm - 1:tm]


def _even_proj(x2, g, mod, w_main, w_vt, w_fg, w_fgt, fb_col, fb_row, cos_t, sin_t, S):
    T, D = x2.shape
    B = T // S
    tm = min(512, S)
    tps = S // tm
    nfv = 2 * LANES * (FOX_HEADS // 2)
    tok = lambda w: pl.BlockSpec((tm, w), lambda i: (i, 0))
    tok_t = lambda w: pl.BlockSpec((w, tm), lambda i: (0, i))
    const = lambda a: pl.BlockSpec(a.shape, lambda i: (0,) * a.ndim)
    return pl.pallas_call(
        functools.partial(_even_proj_kernel, tiles_per_seq=tps),
        grid=(T // tm,),
        in_specs=[tok(D), const(g), pl.BlockSpec((1, 6, D), lambda i: (i // tps, 0, 0)),
                  const(w_main), const(w_vt), const(w_fg), const(w_fgt), const(fb_col), const(fb_row),
                  tok(LANES), tok(LANES)],
        out_specs=[tok(512), tok(1024), tok_t(nfv), tok(LANES),
                   pl.BlockSpec((1, ROWS16, tm), lambda i: (i // tps, 0, i % tps)),
                   tok(512), tok(2 * LANES), tok_t(2 * LANES)],
        out_shape=[jax.ShapeDtypeStruct((T, 512), BF16), jax.ShapeDtypeStruct((T, 1024), BF16),
                   jax.ShapeDtypeStruct((nfv, T), BF16),
                   jax.ShapeDtypeStruct((T, LANES), F32), jax.ShapeDtypeStruct((B, ROWS16, S), F32),
                   jax.ShapeDtypeStruct((T, 512), BF16), jax.ShapeDtypeStruct((T, 2 * LANES), BF16),
                   jax.ShapeDtypeStruct((2 * LANES, T), BF16)],
        scratch_shapes=[pltpu.VMEM((1, LANES), F32), pltpu.VMEM((ROWS16, 1), F32)],
        compiler_params=_params("arbitrary"),
        name="even_proj",
    )(x2, g, mod, w_main, w_vt, w_fg, w_fgt, fb_col, fb_row, cos_t, sin_t)


def _head_step(s_t, vt_h, m_ref, l_ref, head, cq=None):
    m_old = m_ref[head]
    smax = jnp.max(s_t, axis=0, keepdims=True)
    if cq is not None:
        smax = smax + cq
    m_new = jnp.maximum(m_old, smax)
    m_ref[head] = m_new
    shift = m_new if cq is None else m_new - cq
    alpha = jnp.exp2(m_old - m_new)
    pv = _mm(vt_h, jnp.exp2(s_t - shift).astype(BF16))
    sum_row = HEAD_DIM if head % 2 == 0 else 0
    l_ref[head] = alpha * l_ref[head] + pv[sum_row:sum_row + 1]
    return alpha, pv


def _loop_key_blocks(n_blocks, steps_per_block, step):
    big = 2 * steps_per_block

    def body2(i, carry):
        for u in range(big):
            step(i * big + u)
        return carry

    def body1(i, carry):
        for u in range(steps_per_block):
            step((n_blocks // 2) * big + u)
        return carry

    lax.fori_loop(0, n_blocks // 2, body2, 0)
    lax.fori_loop(0, n_blocks % 2, body1, 0)


def _pair_rows(a, b, tq):
    return jnp.concatenate([jnp.broadcast_to(a, (HEAD_DIM, tq)), jnp.broadcast_to(b, (HEAD_DIM, tq))], axis=0)


def _pair_acc(acc_ref, p, alpha_a, pv_a, alpha_b, pv_b):
    tq = pv_a.shape[1]
    pv = jnp.concatenate([pv_a[:HEAD_DIM], pv_b[HEAD_DIM:]], axis=0)
    acc_ref[p] = _pair_rows(alpha_a, alpha_b, tq) * acc_ref[p] + pv


def _vt_blocks(vt_ref, blk, rows):
    base = 2 * blk * LANES
    return vt_ref[base:base + LANES, rows], vt_ref[base + LANES:base + 2 * LANES, rows]


def _init_state(m_ref, l_ref, acc_ref):
    m_ref[...] = jnp.full_like(m_ref, NEG)
    l_ref[...] = jnp.zeros_like(l_ref)
    acc_ref[...] = jnp.zeros_like(acc_ref)


def _finish_pair(l_ref, acc_ref, p):
    tq = acc_ref.shape[2]
    o_t = acc_ref[p] / _pair_rows(l_ref[2 * p], l_ref[2 * p + 1], tq)
    return jnp.transpose(o_t).astype(BF16)


def _attn_scratch(n_heads, tq):
    return [pltpu.VMEM((n_heads, 1, tq), F32), pltpu.VMEM((n_heads, 1, tq), F32),
            pltpu.VMEM((n_heads // 2, LANES, tq), F32)]


def _fox_kernel(q_ref, k_ref, vt_ref, cc_ref, cr_ref, o_ref, m_ref, l_ref, acc_ref, *, tk):
    tq = q_ref.shape[0]
    n_pairs = q_ref.shape[1] // LANES
    q0 = pl.program_id(1) * tq
    n_full = q0 // tk
    qpos = q0 + lax.broadcasted_iota(jnp.int32, (1, tq), 1)
    _init_state(m_ref, l_ref, acc_ref)

    def step(j, masked):
        rows = pl.ds(pl.multiple_of(j * tk, tk), tk)
        if masked:
            causal = (j * tk + lax.broadcasted_iota(jnp.int32, (tk, 1), 0)) <= qpos
        for p in range(n_pairs):
            q2 = q_ref[:, p * LANES:(p + 1) * LANES]
            vts = _vt_blocks(vt_ref, p, rows)
            upd = []
            for half, h in enumerate((2 * p, 2 * p + 1)):
                s_t = _nt(k_ref[rows, h * LANES:(h + 1) * LANES], q2) - cc_ref[rows, h:h + 1]
                if masked:
                    s_t = jnp.where(causal, s_t, NEG)
                upd += _head_step(s_t, vts[half], m_ref, l_ref, h, cr_ref[0, h:h + 1, :])
            _pair_acc(acc_ref, p, *upd)

    _loop_key_blocks(q0 // tq, tq // tk, functools.partial(step, masked=False))
    for d in range(tq // tk):
        step(n_full + d, True)
    for p in range(n_pairs):
        o_ref[:, p * LANES:(p + 1) * LANES] = _finish_pair(l_ref, acc_ref, p)


def _fox_attention(fq, fk, fvt, cum_c, cum_r, S):
    T, W = fq.shape
    B = T // S
    tq = min(256, S)
    nq = S // tq
    return pl.pallas_call(
        functools.partial(_fox_kernel, tk=min(ATTN_KEYS, S)),
        grid=(B, nq),
        in_specs=[pl.BlockSpec((tq, W), lambda b, i: (b * nq + i, 0)),
                  pl.BlockSpec((S, fk.shape[1]), lambda b, i: (b, 0)),
                  pl.BlockSpec((fvt.shape[0], S), lambda b, i: (0, b)),
                  pl.BlockSpec((S, LANES), lambda b, i: (b, 0)),
                  pl.BlockSpec((1, ROWS16, tq), lambda b, i: (b, 0, i))],
        out_specs=pl.BlockSpec((tq, W), lambda b, i: (b * nq + i, 0)),
        out_shape=jax.ShapeDtypeStruct((T, W), BF16),
        scratch_shapes=_attn_scratch(W // HEAD_DIM, tq),
        compiler_params=_params("arbitrary", "arbitrary"),
        name="fox_attention",
    )(fq, fk, fvt, cum_c, cum_r)


def _swa_kernel(sink_ref, q_ref, k_ref, vt_ref, o_ref, m_ref, l_ref, acc_ref, *, tk):
    tq = q_ref.shape[0]
    S = k_ref.shape[0]
    n_pairs = q_ref.shape[1] // LANES
    q0 = pl.program_id(1) * tq
    first = jnp.maximum(q0 - SWA_WINDOW, 0) // tk
    qpos = q0 + lax.broadcasted_iota(jnp.int32, (1, tq), 1)
    _init_state(m_ref, l_ref, acc_ref)
    for d in range(min(tq + SWA_WINDOW, S) // tk):
        j = first + d
        rows = pl.ds(pl.multiple_of(j * tk, tk), tk)
        kpos = j * tk + lax.broadcasted_iota(jnp.int32, (tk, 1), 0)
        valid = (kpos <= qpos) & (qpos - kpos < SWA_WINDOW)
        vts = _vt_blocks(vt_ref, 0, rows)
        for p in range(n_pairs):
            q2 = q_ref[:, p * LANES:(p + 1) * LANES]
            upd = []
            for half in range(2):
                s_t = jnp.where(valid, _nt(k_ref[rows, half * LANES:(half + 1) * LANES], q2), NEG)
                upd += _head_step(s_t, vts[half], m_ref, l_ref, 2 * p + half)
            _pair_acc(acc_ref, p, *upd)
    for p in range(n_pairs):
        scales = []
        for h in (2 * p, 2 * p + 1):
            sink = sink_ref[h] * LOG2E
            m_f = jnp.maximum(m_ref[h], sink)
            scale = jnp.exp2(m_ref[h] - m_f)
            scales.append(scale / (l_ref[h] * scale + jnp.exp2(sink - m_f)))
        o_t = acc_ref[p] * _pair_rows(*scales, tq)
        o_ref[:, p * LANES:(p + 1) * LANES] = jnp.transpose(o_t).astype(BF16)


def _swa_attention(sq, sk, svt, sinks, S):
    T, W = sq.shape
    B = T // S
    tq = min(256, S)
    nq = S // tq
    return pl.pallas_call(
        functools.partial(_swa_kernel, tk=min(ATTN_KEYS, S)),
        grid=(B, nq),
        in_specs=[pl.BlockSpec(memory_space=pltpu.SMEM),
                  pl.BlockSpec((tq, W), lambda b, i: (b * nq + i, 0)),
                  pl.BlockSpec((S, sk.shape[1]), lambda b, i: (b, 0)),
                  pl.BlockSpec((svt.shape[0], S), lambda b, i: (0, b))],
        out_specs=pl.BlockSpec((tq, W), lambda b, i: (b * nq + i, 0)),
        out_shape=jax.ShapeDtypeStruct((T, W), BF16),
        scratch_shapes=_attn_scratch(W // HEAD_DIM, tq),
        compiler_params=_params("arbitrary", "arbitrary"),
        name="swa_attention",
    )(sinks, sq, sk, svt)


RANK_BLOCK = 256


def _post_attn_kernel(*refs, n_in, route):
    x_ref, mod_ref, g_ref = refs[0:3]
    o_refs = refs[3:3 + n_in]
    w_refs = refs[3 + n_in:3 + 2 * n_in]
    pos = 3 + 2 * n_in
    if route:
        rw_ref, rb_ref = refs[pos:pos + 2]
        pos += 2
    x1_ref, h_ref = refs[pos:pos + 2]
    pos += 2
    mix = _mm(o_refs[0][...], w_refs[0][...])
    for o_ref, w_ref in zip(o_refs[1:], w_refs[1:]):
        mix = mix + _mm(o_ref[...], w_ref[...])
    x1 = x_ref[...] + mod_ref[0, 2:3, :] * mix
    x1_ref[...] = x1
    h = _norm_mod(x1, g_ref[...], mod_ref[0, 4:5, :], mod_ref[0, 3:4, :])
    h_ref[...] = h.astype(BF16)
    if not route:
        return
    gater_ref, rankr_ref, cnt_ref = refs[pos:pos + 3]
    tm = h.shape[0]
    E = rw_ref.shape[0]
    lane = lax.broadcasted_iota(jnp.int32, (tm, LANES), 1).astype(F32)
    logits = jnp.full((tm, LANES), -jnp.inf, F32)
    for e in range(E):
        le = jnp.sum(h * rw_ref[e:e + 1, :], axis=1, keepdims=True) + rb_ref[e]
        logits = jnp.where(lane == float(e), le, logits)
    top1 = jnp.max(logits, axis=1, keepdims=True)
    idx1 = jnp.min(jnp.where(logits == top1, lane, float(LANES)), axis=1, keepdims=True)
    rest = jnp.where(lane == idx1, -jnp.inf, logits)
    top2 = jnp.max(rest, axis=1, keepdims=True)
    idx2 = jnp.min(jnp.where(rest == top2, lane, float(LANES)), axis=1, keepdims=True)
    e2 = jnp.exp(top2 - top1)
    g1 = 1.0 / (1.0 + e2)
    g2 = e2 / (1.0 + e2)
    is1 = lane == idx1
    is2 = lane == idx2
    n_rows = gater_ref.shape[0]
    gater_ref[...] = jnp.transpose(jnp.where(is1, g1, 0.0) + jnp.where(is2, g2, 0.0))[0:n_rows, :]
    sel = jnp.where(is1 | is2, 1.0, 0.0)
    rb = min(RANK_BLOCK, tm)
    row = lax.broadcasted_iota(jnp.int32, (rb, rb), 0)
    col = lax.broadcasted_iota(jnp.int32, (rb, rb), 1)
    strict_lower = jnp.where(col < row, 1.0, 0.0).astype(BF16)
    seen = jnp.zeros((1, LANES), F32)
    ranks = []
    for blk in range(tm // rb):
        sel_b = sel[blk * rb:(blk + 1) * rb, :]
        rank = _mm(strict_lower, sel_b.astype(BF16)) + seen
        ranks.append(jnp.where(sel_b > 0.0, rank, -1.0))
        seen = seen + jnp.sum(sel_b, axis=0, keepdims=True)
    rankr_ref[...] = jnp.transpose(jnp.concatenate(ranks, axis=0))[0:n_rows, :]
    cnt_ref[0] = seen.astype(jnp.int32)


def _post_attn(x2, mod, g, os_, ws, S, tm, router=None):
    T, D = x2.shape
    tps = S // tm
    n_in = len(os_)
    tok = lambda w: pl.BlockSpec((tm, w), lambda i: (i, 0))
    const = lambda a: pl.BlockSpec(a.shape, lambda i: (0,) * a.ndim)
    in_specs = [tok(D), pl.BlockSpec((1, 6, D), lambda i: (i // tps, 0, 0)), const(g)]
    in_specs += [tok(o.shape[1]) for o in os_] + [const(w) for w in ws]
    args = [x2, mod, g, *os_, *ws]
    out_specs = [tok(D), tok(D)]
    out_shape = [jax.ShapeDtypeStruct((T, D), F32), jax.ShapeDtypeStruct((T, D), BF16)]
    if router is not None:
        rw_t, rb = router
        in_specs += [const(rw_t), pl.BlockSpec(memory_space=pltpu.SMEM)]
        args += [rw_t, rb]
        nT = T // tm
        out_specs += [pl.BlockSpec((8, tm), lambda i: (0, i)), pl.BlockSpec((8, tm), lambda i: (0, i)),
                      pl.BlockSpec((1, 1, LANES), lambda i: (i, 0, 0))]
        out_shape += [jax.ShapeDtypeStruct((8, T), F32), jax.ShapeDtypeStruct((8, T), F32),
                      jax.ShapeDtypeStruct((nT, 1, LANES), jnp.int32)]
    return pl.pallas_call(
        functools.partial(_post_attn_kernel, n_in=n_in, route=router is not None),
        grid=(T // tm,),
        in_specs=in_specs,
        out_specs=out_specs,
        out_shape=out_shape,
        compiler_params=_params("arbitrary"),
        name="post_attn_route" if router is not None else "post_attn",
    )(*args)


def _ffn_kernel(x1_ref, h_ref, mod_ref, wg_ref, wu_ref, wd_ref, o_ref, acc_ref):
    k = pl.program_id(1)
    h = h_ref[...]
    act = (_silu(_mm(h, wg_ref[...])) * _mm(h, wu_ref[...])).astype(BF16)
    part = _mm(act, wd_ref[...])

    @pl.when(k == 0)
    def _():
        acc_ref[...] = part

    @pl.when(k > 0)
    def _():
        acc_ref[...] += part

    @pl.when(k == pl.num_programs(1) - 1)
    def _():
        o_ref[...] = x1_ref[...] + mod_ref[0, 5:6, :] * acc_ref[...]


def _ffn(x1, h, mod, wg, wu, wd, S):
    T, D = x1.shape
    F = wg.shape[1]
    tm = min(512, S)
    tps = S // tm
    nf = 2
    tf = F // nf
    return pl.pallas_call(
        _ffn_kernel,
        grid=(T // tm, nf),
        in_specs=[pl.BlockSpec((tm, D), lambda i, k: (i, 0)),
                  pl.BlockSpec((tm, D), lambda i, k: (i, 0)),
                  pl.BlockSpec((1, 6, D), lambda i, k: (i // tps, 0, 0)),
                  pl.BlockSpec((D, tf), lambda i, k: (0, k)),
                  pl.BlockSpec((D, tf), lambda i, k: (0, k)),
                  pl.BlockSpec((tf, D), lambda i, k: (k, 0))],
        out_specs=pl.BlockSpec((tm, D), lambda i, k: (i, 0)),
        out_shape=jax.ShapeDtypeStruct((T, D), F32),
        scratch_shapes=[pltpu.VMEM((tm, D), F32)],
        compiler_params=_params("arbitrary", "arbitrary"),
        name="dense_ffn",
    )(x1, h, mod, wg, wu, wd)


O_Q, O_K, O_IQ, O_IK, O_END = 0, 1024, 1280, 1792, 2048
IW_SCALE = IDX_HEADS ** -0.5 * HEAD_DIM ** -0.5


def _odd_proj_kernel(x_ref, g_ref, mod_ref, w_ref, wvt_ref, wiwt_ref, cos_ref, sin_ref,
                     q_ref, k_ref, vt_ref, iq_ref, ik_ref, iwt_ref):
    hb = _norm_mod(x_ref[...], g_ref[...], mod_ref[0, 1:2, :], mod_ref[0, 0:1, :]).astype(BF16)
    rope = (cos_ref[...], sin_ref[...])
    _project(hb, w_ref, O_Q, O_K, q_ref, rope=rope)
    _project(hb, w_ref, O_K, O_IQ, k_ref, rope=rope, split=True)
    _project(hb, w_ref, O_IQ, O_IK, iq_ref, rope=rope)
    _project(hb, w_ref, O_IK, O_END, ik_ref, rope=rope)
    _store_vt(vt_ref, _nt(wvt_ref[...], hb))
    iwt_ref[0] = _nt(wiwt_ref[...], hb) * IW_SCALE


def _odd_proj(x2, g, mod, w_main, w_vt, w_iwt, cos_t, sin_t, S):
    T, D = x2.shape
    B = T // S
    tm = min(512, S)
    tps = S // tm
    tok = lambda w: pl.BlockSpec((tm, w), lambda i: (i, 0))
    const = lambda a: pl.BlockSpec(a.shape, lambda i: (0,) * a.ndim)
    nv = 2 * LANES * (DSA_KV_HEADS // 2)
    nk = 2 * (O_IQ - O_K)
    return pl.pallas_call(
        _odd_proj_kernel,
        grid=(T // tm,),
        in_specs=[tok(D), const(g), pl.BlockSpec((1, 6, D), lambda i: (i // tps, 0, 0)),
                  const(w_main), const(w_vt), const(w_iwt), tok(LANES), tok(LANES)],
        out_specs=[tok(O_K - O_Q), tok(nk), pl.BlockSpec((nv, tm), lambda i: (0, i)),
                   tok(O_IK - O_IQ), tok(O_END - O_IK),
                   pl.BlockSpec((1, ROWS16, tm), lambda i: (i // tps, 0, i % tps))],
        out_shape=[jax.ShapeDtypeStruct((T, O_K - O_Q), BF16), jax.ShapeDtypeStruct((T, nk), BF16),
                   jax.ShapeDtypeStruct((nv, T), BF16),
                   jax.ShapeDtypeStruct((T, O_IK - O_IQ), BF16), jax.ShapeDtypeStruct((T, O_END - O_IK), BF16),
                   jax.ShapeDtypeStruct((B, ROWS16, S), F32)],
        compiler_params=_params("arbitrary"),
        name="odd_proj",
    )(x2, g, mod, w_main, w_vt, w_iwt, cos_t, sin_t)


COUNT_ROWS = 64
HALF16 = 2 ** 15


def _dsa_kernel(q_ref, k_ref, vt_ref, iq_ref, ik_ref, iwt_ref, o_ref,
                key_ref, hi_ref, lo_ref, bias_ref, m_ref, l_ref, acc_ref, *, tk, tka, topk):
    tq = q_ref.shape[0]
    n_pairs = q_ref.shape[1] // LANES
    q0 = pl.program_id(1) * tq
    nkv = (q0 + tq + tk - 1) // tk
    qpos = q0 + lax.broadcasted_iota(jnp.int32, (1, tq), 1)
    kf = float(topk)
    int_min = jnp.int32(INT_MIN)

    def chunk(j):
        return pl.ds(pl.multiple_of(j * tk, tk), tk)

    def score_body(j, carry):
        sc = jnp.zeros((tk, tq), F32)
        for hp in range(iq_ref.shape[1] // LANES):
            iq2 = iq_ref[:, hp * LANES:(hp + 1) * LANES]
            for half in range(2):
                h = 2 * hp + half
                logit = _nt(ik_ref[chunk(j), half * LANES:(half + 1) * LANES], iq2)
                sc = sc + iwt_ref[0, h:h + 1, :] * jnp.maximum(logit, 0.0)
        sc = jnp.where(sc == 0.0, 0.0, sc)
        bits = lax.bitcast_convert_type(sc, jnp.int32)
        key = jnp.where(bits < 0, bits ^ jnp.int32(0x7FFFFFFF), bits)
        causal = (j * tk + lax.broadcasted_iota(jnp.int32, (tk, 1), 0)) <= qpos
        key = jnp.where(causal, key, int_min)
        key_ref[chunk(j), :] = key
        hi_ref[chunk(j), :] = (key >> 16).astype(jnp.int16)
        lo_ref[chunk(j), :] = ((key & 0xFFFF) - HALF16).astype(jnp.int16)
        return carry

    lax.fori_loop(0, nkv, score_body, 0)

    def count_ge16(ref, cand):
        c16 = cand.astype(jnp.int16)

        def body(j, acc):
            ind = jnp.where(ref[chunk(j), :] >= c16, jnp.int16(1), jnp.int16(0))
            for r in range(tk // COUNT_ROWS):
                acc = acc + ind[r * COUNT_ROWS:(r + 1) * COUNT_ROWS]
            return acc
        acc = lax.fori_loop(0, nkv, body, jnp.zeros((COUNT_ROWS, tq), jnp.int16))
        return jnp.sum(acc.astype(F32), axis=0, keepdims=True)

    def search16(ref):
        lowest = jnp.full((1, tq), -HALF16, jnp.int32)
        zero = jnp.zeros((1, tq), jnp.int32)
        t = jnp.where(count_ge16(ref, zero) >= kf, zero, lowest)

        def bit_body(it, t):
            cand = t + jnp.left_shift(jnp.int32(1), 14 - it)
            return jnp.where(count_ge16(ref, cand) >= kf, cand, t)
        return lax.fori_loop(0, 15, bit_body, t)

    t_hi = search16(hi_ref)
    t_hi16 = t_hi.astype(jnp.int16)

    def bucket_body(j, carry):
        hi = hi_ref[chunk(j), :]
        lo_ref[chunk(j), :] = jnp.where(hi > t_hi16, jnp.int16(HALF16 - 1),
                                        jnp.where(hi == t_hi16, lo_ref[chunk(j), :], jnp.int16(-HALF16)))
        return carry

    lax.fori_loop(0, nkv, bucket_body, 0)
    t_lo = search16(lo_ref)
    thr = jnp.left_shift(t_hi, 16) | (t_lo + HALF16)
    def count(pred):
        def body(j, acc):
            ind = jnp.where(pred(key_ref[chunk(j), :]), 1.0, 0.0)
            return acc + jnp.sum(ind.reshape(tk // COUNT_ROWS, COUNT_ROWS, tq), axis=0)
        return jnp.sum(lax.fori_loop(0, nkv, body, jnp.zeros((COUNT_ROWS, tq), F32)), axis=0, keepdims=True)

    n_gt = count(lambda kc: kc > thr)
    n_ge = count(lambda kc: kc >= thr)
    need = kf - n_gt
    has_tie = jnp.max(jnp.where((n_ge > kf) & (thr != int_min), 1.0, 0.0)) > 0.0

    @pl.when(jnp.logical_not(has_tie))
    def _():
        def body(j, carry):
            kc = key_ref[chunk(j), :]
            bias_ref[chunk(j), :] = jnp.where((kc >= thr) & (kc > int_min), 0.0, NEG)
            return carry
        lax.fori_loop(0, nkv, body, 0)

    @pl.when(has_tie)
    def _():
        r = lax.broadcasted_iota(jnp.int32, (tk, tk), 0)
        c = lax.broadcasted_iota(jnp.int32, (tk, tk), 1)
        strict_lower = jnp.where(c < r, 1.0, 0.0).astype(BF16)

        def body(j, seen):
            kc = key_ref[chunk(j), :]
            eq = kc == thr
            eqf = jnp.where(eq, 1.0, 0.0)
            before = _mm(strict_lower, eqf.astype(BF16)) + seen
            sel = ((kc > thr) | (eq & (before < need))) & (kc > int_min)
            bias_ref[chunk(j), :] = jnp.where(sel, 0.0, NEG)
            return seen + jnp.sum(eqf, axis=0, keepdims=True)
        lax.fori_loop(0, nkv, body, jnp.zeros((1, tq), F32))

    pairs_per_block = n_pairs // (k_ref.shape[1] // (2 * LANES))
    _init_state(m_ref, l_ref, acc_ref)

    def attn_step(j):
        rows = pl.ds(pl.multiple_of(j * tka, tka), tka)
        bias = bias_ref[rows, :]
        for p in range(n_pairs):
            blk = p // pairs_per_block
            q2 = q_ref[:, p * LANES:(p + 1) * LANES]
            vts = _vt_blocks(vt_ref, blk, rows)
            upd = []
            for half in range(2):
                kh = 2 * blk + half
                s_t = _nt(k_ref[rows, kh * LANES:(kh + 1) * LANES], q2) + bias
                upd += _head_step(s_t, vts[half], m_ref, l_ref, 2 * p + half)
            _pair_acc(acc_ref, p, *upd)

    _loop_key_blocks((q0 + tq) // tq, tq // tka, attn_step)
    for p in range(n_pairs):
        o_ref[:, p * LANES:(p + 1) * LANES] = _finish_pair(l_ref, acc_ref, p)


def _dsa_attention(q, k, vt, iq, ik, iwt, S):
    T, W = q.shape
    B = T // S
    tq = min(256, S)
    tk = min(512, S)
    nq = S // tq
    topk = min(DSA_TOPK_MAX, S // 4)
    qtok = lambda w: pl.BlockSpec((tq, w), lambda b, i: (b * nq + i, 0))
    seq = lambda w: pl.BlockSpec((S, w), lambda b, i: (b, 0))
    return pl.pallas_call(
        functools.partial(_dsa_kernel, tk=tk, tka=min(ATTN_KEYS, S), topk=topk),
        grid=(B, nq),
        in_specs=[qtok(W), seq(k.shape[1]), pl.BlockSpec((vt.shape[0], S), lambda b, i: (0, b)),
                  qtok(iq.shape[1]), seq(ik.shape[1]), pl.BlockSpec((1, ROWS16, tq), lambda b, i: (b, 0, i))],
        out_specs=qtok(W),
        out_shape=jax.ShapeDtypeStruct((T, W), BF16),
        scratch_shapes=[pltpu.VMEM((S, tq), jnp.int32), pltpu.VMEM((S, tq), jnp.int16), pltpu.VMEM((S, tq), jnp.int16),
                        pltpu.VMEM((S, tq), F32)] + _attn_scratch(W // HEAD_DIM, tq),
        compiler_params=_params("arbitrary", "arbitrary"),
        name="dsa_attention",
    )(q, k, vt, iq, ik, iwt)


MOE_ROWS = 128


def _moe_kernel(cnt_ref, x1_ref, h_ref, gater_ref, rankr_ref, mod_ref, wgu_ref, wd_ref, fn_ref, o_ref, y_ref):
    i = pl.program_id(0)
    e = pl.program_id(1)
    E = pl.num_programs(1)
    F = wd_ref.shape[1]
    M = MOE_ROWS

    @pl.when(e == 0)
    def _():
        o_ref[...] = x1_ref[...]

    rank_row = rankr_ref[pl.ds(e, 1), :]
    gate_row = gater_ref[pl.ds(e, 1), :]
    g_f = mod_ref[0, 5:6, :]
    slot_col = lax.broadcasted_iota(jnp.int32, (2 * M, 1), 0).astype(F32)
    n_steps = (cnt_ref[i * E + e] + M - 1) // M

    def expert_rows(hit):
        xg = _mm(jnp.where(hit, 1.0, 0.0).astype(BF16), h_ref[...]).astype(BF16)
        gu = _mm(xg, wgu_ref[0])
        act = (_silu(gu[:, :F]) * gu[:, F:]).astype(BF16)
        return (_mm(act, wd_ref[0]) * g_f).astype(BF16)

    def body(c, carry):
        hit = rank_row == (c * 2 * M).astype(F32) + slot_col

        @pl.when(2 * c + 1 < n_steps)
        def _():
            y_ref[...] = expert_rows(hit)

        @pl.when(2 * c + 1 >= n_steps)
        def _():
            y_ref[0:M] = expert_rows(hit[0:M])
            y_ref[M:2 * M] = jnp.zeros((M, y_ref.shape[1]), BF16)

        scatter_t = jnp.where(hit, gate_row, 0.0).astype(BF16)
        o_ref[...] += lax.dot_general(scatter_t, y_ref[...], (((0,), (0,)), ((), ())), preferred_element_type=F32)
        return carry

    lax.fori_loop(0, (n_steps + 1) // 2, body, 0)

    @pl.when(e == E - 1)
    def _():
        x = o_ref[...]
        o_ref[...] = x * lax.rsqrt(jnp.mean(x * x, axis=-1, keepdims=True) + NORM_EPS) * fn_ref[...]


def _moe(x1, h, gate_r, rank_r, counts, mod, w_gu, w_d, final_g, S, tm):
    T, D = x1.shape
    E = w_gu.shape[0]
    tps = S // tm
    grid_spec = pltpu.PrefetchScalarGridSpec(
        num_scalar_prefetch=1,
        grid=(T // tm, E),
        in_specs=[pl.BlockSpec((tm, D), lambda i, e, c: (i, 0)),
                  pl.BlockSpec((tm, D), lambda i, e, c: (i, 0)),
                  pl.BlockSpec((8, tm), lambda i, e, c: (0, i)),
                  pl.BlockSpec((8, tm), lambda i, e, c: (0, i)),
                  pl.BlockSpec((1, 6, D), lambda i, e, c: (i // tps, 0, 0)),
                  pl.BlockSpec((1,) + w_gu.shape[1:], lambda i, e, c: (e, 0, 0)),
                  pl.BlockSpec((1,) + w_d.shape[1:], lambda i, e, c: (e, 0, 0)),
                  pl.BlockSpec((1, D), lambda i, e, c: (0, 0))],
        out_specs=pl.BlockSpec((tm, D), lambda i, e, c: (i, 0)),
        scratch_shapes=[pltpu.VMEM((2 * MOE_ROWS, D), BF16)],
    )
    return pl.pallas_call(
        _moe_kernel,
        grid_spec=grid_spec,
        out_shape=jax.ShapeDtypeStruct((T, D), F32),
        compiler_params=_params("arbitrary", "arbitrary"),
        name="moe_ffn",
    )(counts, x1, h, gate_r, rank_r, mod, w_gu, w_d, final_g)


def _pair_order(n_heads, n_kv):
    group = n_heads // n_kv
    order = []
    for j in range(n_kv // 2):
        for i in range(group):
            order += [(2 * j) * group + i, (2 * j + 1) * group + i]
    return np.asarray(order)


def _head_cols(order):
    return (order[:, None] * HEAD_DIM + np.arange(HEAD_DIM)[None, :]).reshape(-1)


def _pad_rows(a, rows):
    return jnp.pad(a, ((0, rows - a.shape[0]), (0, 0)))


def kernel(x, c, positions, e_ada_w, e_ada_b, e_norm_mix, e_norm_ffn, e_w_in, e_forget_b, e_sinks, e_w_out, e_ffn_gate, e_ffn_up, e_ffn_down, o_ada_w, o_ada_b, o_norm_mix, o_norm_ffn, o_w_in, o_w_out, o_router_w, o_router_b, o_exp_gate, o_exp_up, o_exp_down, final_norm):
    B, S, D = x.shape
    T = B * S
    q_scale = HEAD_DIM ** -0.5 * LOG2E
    x2 = x.reshape(T, D)
    cos_t, sin_t = _rope_tables(positions)

    li = 0
    mod = _ada(c, e_ada_w[li], e_ada_b[li])
    w = e_w_in[li]
    nf = FOX_HEADS * HEAD_DIM
    c_fq, c_fk, c_fv, c_fg = 0, nf, 2 * nf, 3 * nf
    c_sq = c_fg + FOX_HEADS
    c_sk = c_sq + SWA_HEADS * HEAD_DIM
    c_sv = c_sk + SWA_KV_HEADS * HEAD_DIM
    swa_order = _pair_order(SWA_HEADS, SWA_KV_HEADS)
    swa_cols = _head_cols(swa_order)
    w_main = jnp.concatenate([w[:, c_fq:c_fk] * q_scale, w[:, c_fk:c_fv],
                              w[:, c_sq:c_sk][:, swa_cols] * q_scale, w[:, c_sk:c_sv]], axis=1).astype(BF16)
    w_vt = jnp.concatenate([w[:, c_fv:c_fg], w[:, c_sv:]], axis=1).T.astype(BF16)
    w_fg = jnp.pad(w[:, c_fg:c_sq], ((0, 0), (0, LANES - FOX_HEADS))).astype(BF16)
    w_fgt = _pad_rows(w[:, c_fg:c_sq].T, ROWS16).astype(BF16)
    fb_col = jnp.pad(e_forget_b[li], (0, LANES - FOX_HEADS))[None, :]
    fb_row = jnp.pad(e_forget_b[li], (0, ROWS16 - FOX_HEADS))[:, None]
    fq, fk, fvt, cum_c, cum_r, sq, sk, svt = _even_proj(
        x2, e_norm_mix[li][None, :], mod, w_main, w_vt, w_fg, w_fgt, fb_col, fb_row, cos_t, sin_t, S)
    o_fox = _fox_attention(fq, fk, fvt, cum_c, cum_r, S)
    o_swa = _swa_attention(sq, sk, svt, e_sinks[li][swa_order], S)
    wo = e_w_out[li]
    x1, h = _post_attn(x2, mod, e_norm_ffn[li][None, :], [o_fox, o_swa],
                       [wo[:nf].astype(BF16), wo[nf:][swa_cols].astype(BF16)], S, min(512, S))
    x2 = _ffn(x1, h, mod, e_ffn_gate[li].astype(BF16), e_ffn_up[li].astype(BF16), e_ffn_down[li].astype(BF16), S)

    mod = _ada(c, o_ada_w[li], o_ada_b[li])
    w = o_w_in[li]
    c_k = DSA_HEADS * HEAD_DIM
    c_v = c_k + DSA_KV_HEADS * HEAD_DIM
    c_iq = c_v + DSA_KV_HEADS * HEAD_DIM
    c_ik = c_iq + IDX_HEADS * HEAD_DIM
    c_iw = c_ik + HEAD_DIM
    dsa_cols = _head_cols(_pair_order(DSA_HEADS, DSA_KV_HEADS))
    w_ik = w[:, c_ik:c_iw]
    no_ik = jnp.zeros_like(w_ik)
    w_main = jnp.concatenate([w[:, :c_k][:, dsa_cols] * q_scale, w[:, c_k:c_v], w[:, c_iq:c_ik],
                              w_ik, no_ik, no_ik, w_ik], axis=1).astype(BF16)
    w_vt = w[:, c_v:c_iq].T.astype(BF16)
    w_iwt = _pad_rows(w[:, c_iw:].T, ROWS16).astype(BF16)
    q, k, vt, iq, ik, iwt = _odd_proj(x2, o_norm_mix[li][None, :], mod, w_main, w_vt, w_iwt, cos_t, sin_t, S)
    o_dsa = _dsa_attention(q, k, vt, iq, ik, iwt, S)
    tm_moe = min(1024, S)
    x1, h, gate_r, rank_r, counts = _post_attn(
        x2, mod, o_norm_ffn[li][None, :], [o_dsa], [o_w_out[li][dsa_cols].astype(BF16)], S, tm_moe,
        router=(o_router_w[li].T, o_router_b[li]))
    w_gu = jnp.concatenate([o_exp_gate[li], o_exp_up[li]], axis=2).astype(BF16)
    out = _moe(x1, h, gate_r, rank_r, counts[:, 0, :N_EXPERTS].reshape(-1), mod,
               w_gu, o_exp_down[li].astype(BF16), final_norm[None, :], S, tm_moe)
    return out.reshape(B, S, D)
```
